```python
import math
import jax, jax.numpy as jnp
from jax import lax
import numpy as np

D_MODEL = 4096
BATCH = 32
SEQ = 256
DEPTH = 1
DEC_BATCH = 4
DEC_SEQ = 1024
PAST_LEN = 256

GRID_W = 64
MLA_HEADS = 16
NOPE_DIM = 128
ROPE_DIM = 64
V_DIM = 128
Q_RANK = 768
KV_RANK = 512
MLA_WIDTH = MLA_HEADS * V_DIM
CONV_WIDTH = D_MODEL - MLA_WIDTH
CONV_KERNEL = 31
MIX_WIDTH = MLA_WIDTH + CONV_WIDTH
IN_WIDTH = Q_RANK + KV_RANK + ROPE_DIM + 2 * CONV_WIDTH
PEER_HEADS = 8
PEER_KEYS = 128
PEER_EXPERTS = PEER_KEYS * PEER_KEYS
PEER_QDIM = 512
PEER_HALF = PEER_QDIM // 2
PEER_TOPK = 16
ROPE_BASE = 10000.0
Q_BLOCK = 128
TOKEN_BLOCK = 128
ALPHA = (2 * DEPTH) ** 0.25
BETA = (8 * DEPTH) ** -0.25
EPS = 1e-6

kernel_name = "hybrid_mla_conformer_peer_diffusion_step"


def layer_norm(x, g, b):
    xf = x.astype(jnp.float32)
    mu = jnp.mean(xf, -1, keepdims=True)
    var = jnp.mean(jnp.square(xf - mu), -1, keepdims=True)
    return ((xf - mu) * lax.rsqrt(var + EPS)).astype(x.dtype) * g + b


def rms_norm(x, g):
    xf = x.astype(jnp.float32)
    return (xf * lax.rsqrt(jnp.mean(xf * xf, -1, keepdims=True) + EPS)).astype(x.dtype) * g


def adaln(cond, w_ada, b_ada):
    m = jnp.einsum('bd,dn->bn', jax.nn.silu(cond), w_ada) + b_ada
    return [t[:, None, :] for t in jnp.split(m, 6, axis=-1)]


def axial_rope_tables(n_tokens):
    rows = n_tokens // GRID_W
    row = jnp.repeat(jnp.arange(rows, dtype=jnp.float32), GRID_W)
    col = jnp.tile(jnp.arange(GRID_W, dtype=jnp.float32), rows)
    n_freq = ROPE_DIM // 4
    inv = ROPE_BASE ** (-jnp.arange(n_freq, dtype=jnp.float32) / n_freq)
    ang_r = row[:, None] * inv
    ang_c = col[:, None] * inv
    ang = jnp.concatenate([ang_r, ang_r, ang_c, ang_c], -1)
    return jnp.cos(ang), jnp.sin(ang)


def rotate_half_axial(x):
    def rh(y):
        y1, y2 = jnp.split(y, 2, -1)
        return jnp.concatenate([-y2, y1], -1)
    x_r, x_c = jnp.split(x, 2, -1)
    return jnp.concatenate([rh(x_r), rh(x_c)], -1)


def apply_rope(x, cos, sin):
    return x * cos.astype(x.dtype) + rotate_half_axial(x) * sin.astype(x.dtype)


def expand_kv(ckv, w_ukv):
    B, S, _ = ckv.shape
    kv = jnp.einsum('bsr,rn->bsn', ckv, w_ukv).reshape(B, S, MLA_HEADS, NOPE_DIM + V_DIM)
    return kv[..., :NOPE_DIM], kv[..., NOPE_DIM:]


def mla_attend(q_nope, q_pe, k_nope, k_pe, v):
    B, T = q_nope.shape[:2]
    nb = T // Q_BLOCK
    scale = (NOPE_DIM + ROPE_DIM) ** -0.5

    def to_blocks(a):
        return jnp.moveaxis(a.reshape(B, nb, Q_BLOCK, *a.shape[2:]), 1, 0)

    def block(args):
        qn, qp = args
        s = (jnp.einsum('bqhd,bkhd->bhqk', qn, k_nope, preferred_element_type=jnp.float32)
             + jnp.einsum('bqhd,bkd->bhqk', qp, k_pe, preferred_element_type=jnp.float32))
        p = jax.nn.softmax(s * scale, axis=-1).astype(v.dtype)
        return jnp.einsum('bhqk,bkhd->bqhd', p, v)

    out = lax.map(block, (to_blocks(q_nope), to_blocks(q_pe)))
    return jnp.moveaxis(out, 0, 1).reshape(B, T, MLA_HEADS * V_DIM)


def conv_module(u, w_dw, b_dw, g_cn, b_cn):
    a, gate = jnp.split(u, 2, -1)
    y = a * jax.nn.sigmoid(gate)
    y = lax.conv_general_dilated(
        y, w_dw[:, None, :], window_strides=(1,),
        padding=[(CONV_KERNEL // 2, CONV_KERNEL // 2)],
        dimension_numbers=('NWC', 'WIO', 'NWC'),
        feature_group_count=CONV_WIDTH) + b_dw
    return jax.nn.silu(layer_norm(y, g_cn, b_cn))


def mixing_sublayer(h, w_in, g_q, w_uq, g_kv, w_ukv, w_dw, b_dw, g_cn, b_cn, w_out,
                    rope=None, ctx_ckv=None, ctx_kpe=None):
    B, T, _ = h.shape
    proj = jnp.einsum('btd,dn->btn', h, w_in)
    o1 = Q_RANK
    o2 = o1 + KV_RANK
    o3 = o2 + ROPE_DIM
    q_c = proj[..., :o1]
    ckv = rms_norm(proj[..., o1:o2], g_kv)
    kpe = proj[..., o2:o3]
    u = proj[..., o3:]
    q = jnp.einsum('btr,rn->btn', rms_norm(q_c, g_q), w_uq).reshape(
        B, T, MLA_HEADS, NOPE_DIM + ROPE_DIM)
    q_nope, q_pe = q[..., :NOPE_DIM], q[..., NOPE_DIM:]
    k_nope, v = expand_kv(ckv, w_ukv)
    k_pe = kpe
    if rope is not None:
        cos, sin = rope
        q_pe = apply_rope(q_pe, cos[:, None, :], sin[:, None, :])
        k_pe = apply_rope(kpe, cos, sin)
        ck_nope, cv = expand_kv(ctx_ckv, w_ukv)
        k_nope = jnp.concatenate([ck_nope, k_nope], axis=1)
        v = jnp.concatenate([cv, v], axis=1)
        k_pe = jnp.concatenate([ctx_kpe, k_pe], axis=1)
    attn = mla_attend(q_nope, q_pe, k_nope, k_pe, v)
    conv = conv_module(u, w_dw, b_dw, g_cn, b_cn)
    out = jnp.einsum('btm,md->btd', jnp.concatenate([attn, conv], -1), w_out)
    return out, ckv, kpe


def peer_ffn(h, w_pq, sub_keys, peer_u, peer_v):
    B, T, D = h.shape
    xt = h.reshape(-1, TOKEN_BLOCK, D)

    def block(xb):
        q = jnp.einsum('nd,dq->nq', xb, w_pq).reshape(TOKEN_BLOCK, PEER_HEADS, 2, PEER_HALF)
        s = jnp.einsum('nhpd,hpkd->nhpk', q, sub_keys, preferred_element_type=jnp.float32)
        s1, i1 = lax.top_k(s[:, :, 0], PEER_TOPK)
        s2, i2 = lax.top_k(s[:, :, 1], PEER_TOPK)
        cand = (s1[..., :, None] + s2[..., None, :]).reshape(TOKEN_BLOCK, PEER_HEADS, -1)
        cidx = (i1[..., :, None] * PEER_KEYS + i2[..., None, :]).reshape(TOKEN_BLOCK, PEER_HEADS, -1)
        top_s, pos = lax.top_k(cand, PEER_TOPK)
        expert = jnp.take_along_axis(cidx, pos, axis=-1).reshape(TOKEN_BLOCK, -1)
        g = jax.nn.softmax(top_s, axis=-1).astype(xb.dtype).reshape(TOKEN_BLOCK, -1)
        a = jax.nn.gelu(jnp.einsum('nd,ned->ne', xb, peer_u[expert]), approximate=False)
        return jnp.einsum('ne,ned->nd', g * a, peer_v[expert])

    return lax.map(block, xt).reshape(B, T, D)


def setup_inputs(seed: int = 0) -> dict:
    key = jax.random.key(seed)
    ks = jax.random.split(key, 32)
    f32 = jnp.float32

    def nrm(k, shape, scale):
        return jax.random.normal(k, shape, f32) * scale

    def gain(k, shape):
        return 1.0 + 0.02 * jax.random.normal(k, shape, f32)

    L = DEPTH
    return {
        "x_prompt": nrm(ks[0], (BATCH, SEQ, D_MODEL), 1.0),
        "x_sample": nrm(ks[1], (DEC_BATCH, DEC_SEQ, D_MODEL), 1.0),
        "cache_ckv": nrm(ks[2], (DEC_BATCH, DEPTH, PAST_LEN, KV_RANK), 1.0),
        "cache_kpe": nrm(ks[3], (DEC_BATCH, DEPTH, PAST_LEN, ROPE_DIM), 1.0),
        "c": nrm(ks[4], (DEC_BATCH, D_MODEL), 1.0),
        "c_ctx": nrm(ks[5], (D_MODEL,), 1.0),
        "w_ada": nrm(ks[6], (L, D_MODEL, 6 * D_MODEL), D_MODEL ** -0.5),
        "b_ada": nrm(ks[7], (L, 6 * D_MODEL), 0.02),
        "w_in": nrm(ks[8], (L, D_MODEL, IN_WIDTH), D_MODEL ** -0.5),
        "g_q": gain(ks[9], (L, Q_RANK)),
        "w_uq": nrm(ks[10], (L, Q_RANK, MLA_HEADS * (NOPE_DIM + ROPE_DIM)), Q_RANK ** -0.5),
        "g_kv": gain(ks[11], (L, KV_RANK)),
        "w_ukv": nrm(ks[12], (L, KV_RANK, MLA_HEADS * (NOPE_DIM + V_DIM)), KV_RANK ** -0.5),
        "w_dw": nrm(ks[13], (L, CONV_KERNEL, CONV_WIDTH), CONV_KERNEL ** -0.5),
        "b_dw": nrm(ks[14], (L, CONV_WIDTH), 0.02),
        "g_cn": gain(ks[15], (L, CONV_WIDTH)),
        "b_cn": nrm(ks[16], (L, CONV_WIDTH), 0.02),
        "w_out": nrm(ks[17], (L, MIX_WIDTH, D_MODEL), BETA * MIX_WIDTH ** -0.5),
        "ln1_g": gain(ks[18], (L, D_MODEL)),
        "ln1_b": nrm(ks[19], (L, D_MODEL), 0.02),
        "w_pq": nrm(ks[20], (L, D_MODEL, PEER_HEADS * PEER_QDIM), D_MODEL ** -0.5),
        "sub_keys": nrm(ks[21], (L, PEER_HEADS, 2, PEER_KEYS, PEER_HALF), PEER_HALF ** -0.5),
        "peer_u": nrm(ks[22], (L, PEER_EXPERTS, D_MODEL), D_MODEL ** -0.5),
        "peer_v": nrm(ks[23], (L, PEER_EXPERTS, D_MODEL), BETA * PEER_HEADS ** -0.5),
        "ln2_g": gain(ks[24], (L, D_MODEL)),
        "ln2_b": nrm(ks[25], (L, D_MODEL), 0.02),
    }


def reference(x_prompt, x_sample, cache_ckv, cache_kpe, c, c_ctx,
              w_ada, b_ada, w_in, g_q, w_uq, g_kv, w_ukv, w_dw, b_dw, g_cn, b_cn,
              w_out, ln1_g, ln1_b, w_pq, sub_keys, peer_u, peer_v, ln2_g, ln2_b):
    rope = axial_rope_tables(x_sample.shape[1])
    xp = x_prompt
    xs = x_sample
    ckv_layers = []
    kpe_layers = []
    for l in range(DEPTH):
        mix_w = (w_in[l], g_q[l], w_uq[l], g_kv[l], w_ukv[l], w_dw[l], b_dw[l],
                 g_cn[l], b_cn[l], w_out[l])
        sh1, sc1, g1, sh2, sc2, g2 = adaln(c_ctx[None, :], w_ada[l], b_ada[l])
        mix, ckv, kpe = mixing_sublayer(xp * (1 + sc1) + sh1, *mix_w)
        ckv_layers.append(ckv)
        kpe_layers.append(kpe)
        xp = layer_norm(ALPHA * xp + g1 * mix, ln1_g[l], ln1_b[l])
        ff = peer_ffn(xp * (1 + sc2) + sh2, w_pq[l], sub_keys[l], peer_u[l], peer_v[l])
        xp = layer_norm(ALPHA * xp + g2 * ff, ln2_g[l], ln2_b[l])
        sh1, sc1, g1, sh2, sc2, g2 = adaln(c, w_ada[l], b_ada[l])
        mix, _, _ = mixing_sublayer(xs * (1 + sc1) + sh1, *mix_w, rope=rope,
                                    ctx_ckv=cache_ckv[:, l], ctx_kpe=cache_kpe[:, l])
        xs = layer_norm(ALPHA * xs + g1 * mix, ln1_g[l], ln1_b[l])
        ff = peer_ffn(xs * (1 + sc2) + sh2, w_pq[l], sub_keys[l], peer_u[l], peer_v[l])
        xs = layer_norm(ALPHA * xs + g2 * ff, ln2_g[l], ln2_b[l])
    new_ckv = jnp.stack(ckv_layers, axis=1)
    new_kpe = jnp.stack(kpe_layers, axis=1)
    return (xp, xs, new_ckv, new_kpe)
```

```python
import functools
import math

import jax
import jax.numpy as jnp
from jax import lax
from jax.experimental import pallas as pl
from jax.experimental.pallas import tpu as pltpu

F32 = jnp.float32
BF = jnp.bfloat16

NOPE_DIM = 128
ROPE_DIM = 64
V_DIM = 128
HEAD_Q = NOPE_DIM + 2 * ROPE_DIM
HEAD_KV = NOPE_DIM + V_DIM
GRID_W = 64
ROPE_BASE = 10000.0
PEER_TOPK = 16
EPS = 1e-6

LANES = 128
SUBLANES = 8
VMEM_MIB = 1 << 20

TOKEN_TILE = 512
PEER_TOKEN_TILE = 512
PEER_EXPERT_TILE = 256
ATTN_Q_TILE = 256
CONV_T_TILE = 256
CONV_T_CHUNK = 32
CONV_C_CHUNK = 256
CONV_HALO = 16


def _cp(sem, vmem_mib):
    return pltpu.CompilerParams(dimension_semantics=sem, vmem_limit_bytes=vmem_mib * VMEM_MIB)


def _rotate_half_axial(x):
    q = x.shape[-1] // 4
    a, b, c, d = (x[..., i * q:(i + 1) * q] for i in range(4))
    return jnp.concatenate([-b, a, -d, c], -1)


def _layer_norm_rows(z, g, b):
    mu = jnp.mean(z, -1, keepdims=True)
    zc = z - mu
    var = jnp.mean(zc * zc, -1, keepdims=True)
    return zc * lax.rsqrt(var + EPS) * g + b


def _rms_norm_rows(z, g):
    return z * lax.rsqrt(jnp.mean(z * z, -1, keepdims=True) + EPS) * g


def _adaln_kernel(c_ref, w_ref, b_ref, o_ref):
    c = c_ref[...]
    s = (c * jax.nn.sigmoid(c)).astype(BF)
    o_ref[...] = jnp.dot(s, w_ref[...].astype(BF), preferred_element_type=F32) + b_ref[...]


def _adaln(cond, w, b):
    R, D = cond.shape
    N = w.shape[1]
    tn = min(512, N)
    return pl.pallas_call(
        _adaln_kernel,
        grid=(N // tn,),
        in_specs=[pl.BlockSpec((R, D), lambda j: (0, 0)),
                  pl.BlockSpec((D, tn), lambda j: (0, j)),
                  pl.BlockSpec((1, tn), lambda j: (0, j))],
        out_specs=pl.BlockSpec((R, tn), lambda j: (0, j)),
        out_shape=jax.ShapeDtypeStruct((R, N), F32),
        compiler_params=_cp(("arbitrary",), 40),
        name="adaln",
    )(cond, w, b.reshape(1, N))


def _inproj_a_kernel(x_ref, sc_ref, sh_ref, w_ref, gq_ref, gkv_ref, rope_ref,
                     qcn_ref, ckv_ref, kpe_ref, kpad_ref, *, qr, kvr):
    h = (x_ref[...] * (1.0 + sc_ref[0]) + sh_ref[0]).astype(BF)
    p = jnp.dot(h, w_ref[...], preferred_element_type=F32)
    qcn_ref[...] = _rms_norm_rows(p[:, :qr], gq_ref[...]).astype(BF)
    ckv_ref[...] = _rms_norm_rows(p[:, qr:qr + kvr], gkv_ref[...])
    kslab = p[:, qr + kvr:qr + kvr + 2 * ROPE_DIM]
    kpe_ref[...] = kslab[:, :ROPE_DIM]
    u = kslab * rope_ref[...]
    kr = u + pltpu.roll(u, ROPE_DIM, 1)
    lane = lax.broadcasted_iota(jnp.int32, kr.shape, 1)
    kpad_ref[...] = jnp.where(lane < ROPE_DIM, kr, 0.0).astype(BF)


def _inproj_u_kernel(x_ref, sc_ref, sh_ref, wa_ref, wg_ref, y_ref, h_scr):
    @pl.when(pl.program_id(1) == 0)
    def _():
        h_scr[...] = (x_ref[...] * (1.0 + sc_ref[0]) + sh_ref[0]).astype(BF)

    h = h_scr[...]
    a = jnp.dot(h, wa_ref[...], preferred_element_type=F32)
    g = jnp.dot(h, wg_ref[...], preferred_element_type=F32)
    y_ref[...] = a * jax.nn.sigmoid(g)


def _qproj_kernel(x_ref, w_ref, rope_ref, o_ref, *, heads):
    x = x_ref[...]
    rope = rope_ref[...]
    for h in range(heads):
        b = h * HEAD_Q
        q = jnp.dot(x, w_ref[:, b:b + HEAD_Q], preferred_element_type=F32)
        o_ref[:, b:b + NOPE_DIM] = q[:, :NOPE_DIM].astype(BF)
        u = q[:, NOPE_DIM:] * rope
        o_ref[:, b + NOPE_DIM:b + HEAD_Q] = (u + pltpu.roll(u, ROPE_DIM, 1)).astype(BF)


def _kvproj_kernel(x_ref, w_ref, o_ref, *, heads):
    x = x_ref[...].astype(BF)
    for h in range(heads):
        b = h * HEAD_KV
        o_ref[:, b:b + HEAD_KV] = jnp.dot(
            x, w_ref[:, b:b + HEAD_KV], preferred_element_type=F32).astype(BF)


def _attn_kernel(q_ref, kv_ref, kpad_ref, o_ref, *, heads, scale):
    kpad = kpad_ref[0]
    for h in range(heads):
        qh = q_ref[0, :, h * HEAD_Q:(h + 1) * HEAD_Q]
        kh = jnp.concatenate([kv_ref[0, :, h * HEAD_KV:h * HEAD_KV + NOPE_DIM], kpad], axis=1)
        s = lax.dot_general(qh, kh, (((1,), (1,)), ((), ())),
                            preferred_element_type=F32) * scale
        m = jnp.max(s, -1, keepdims=True)
        p = jnp.exp(s - m)
        l = jnp.sum(p, -1, keepdims=True)
        o = jnp.dot(p.astype(BF), kv_ref[0, :, h * HEAD_KV + NOPE_DIM:(h + 1) * HEAD_KV],
                    preferred_element_type=F32)
        o_ref[0, :, h * V_DIM:(h + 1) * V_DIM] = (o / l).astype(BF)


def _conv_kernel(y_ref, w_ref, bdw_ref, g_ref, b_ref, o_ref, pad_scr, z_scr, *, tt, taps):
    t = pl.program_id(1)
    nt = pl.num_programs(1)
    C = z_scr.shape[1]
    half = taps // 2
    pad_scr[CONV_HALO:CONV_HALO + tt, :] = y_ref[0, pl.ds(pl.multiple_of(t * tt, tt), tt), :]

    @pl.when(t == 0)
    def _():
        pad_scr[0:CONV_HALO, :] = jnp.zeros((CONV_HALO, C), F32)

    @pl.when(t > 0)
    def _():
        pad_scr[0:CONV_HALO, :] = y_ref[
            0, pl.ds(pl.multiple_of(t * tt - CONV_HALO, SUBLANES), CONV_HALO), :]

    @pl.when(t == nt - 1)
    def _():
        pad_scr[CONV_HALO + tt:, :] = jnp.zeros((CONV_HALO, C), F32)

    @pl.when(t < nt - 1)
    def _():
        pad_scr[CONV_HALO + tt:, :] = y_ref[
            0, pl.ds(pl.multiple_of((t + 1) * tt, tt), CONV_HALO), :]

    tc = CONV_T_CHUNK
    cc = min(CONV_C_CHUNK, C)

    def chan_body(ci, carry):
        c0 = pl.multiple_of(ci * cc, cc)
        for r in range(tt // tc):
            acc = jnp.zeros((tc, cc), F32)
            for k in range(taps):
                row = r * tc + CONV_HALO - half + k
                acc = acc + pad_scr[row:row + tc, pl.ds(c0, cc)] * w_ref[k:k + 1, pl.ds(c0, cc)]
            z_scr[r * tc:(r + 1) * tc, pl.ds(c0, cc)] = acc + bdw_ref[:, pl.ds(c0, cc)]
        return carry

    lax.fori_loop(0, C // cc, chan_body, 0)

    rows = 64
    for r in range(tt // rows):
        zn = _layer_norm_rows(z_scr[r * rows:(r + 1) * rows, :], g_ref[...], b_ref[...])
        o_ref[0, r * rows:(r + 1) * rows, :] = (zn * jax.nn.sigmoid(zn)).astype(BF)


def _ln_inplace(xo_ref, x_ref, gate_ref, lg_ref, lb_ref, ff_ref, alpha):
    rows = 64
    for r in range(xo_ref.shape[0] // rows):
        sl = slice(r * rows, (r + 1) * rows)
        z = alpha * x_ref[sl, :] + gate_ref[0] * ff_ref[sl, :]
        xo_ref[sl, :] = _layer_norm_rows(z, lg_ref[...], lb_ref[...])


def _outproj_kernel(a_ref, c_ref, w_ref, x_ref, g1_ref, lg_ref, lb_ref, xo_ref, *, na, nk, alpha):
    k = pl.program_id(1)

    @pl.when(k == 0)
    def _():
        xo_ref[...] = jnp.zeros(xo_ref.shape, F32)

    def accumulate(lhs_ref):
        lhs = lhs_ref[...]
        nc = min(512, xo_ref.shape[1])
        for n in range(xo_ref.shape[1] // nc):
            cs = slice(n * nc, (n + 1) * nc)
            xo_ref[:, cs] += jnp.dot(lhs, w_ref[:, cs], preferred_element_type=F32)

    @pl.when(k < na)
    def _():
        accumulate(a_ref)

    @pl.when(k >= na)
    def _():
        accumulate(c_ref)

    @pl.when(k == nk - 1)
    def _():
        _ln_inplace(xo_ref, x_ref, g1_ref, lg_ref, lb_ref, xo_ref, alpha)


def _peer_q_kernel(x_ref, sc_ref, sh_ref, w_ref, k_ref, h_ref, s_ref):
    @pl.when(pl.program_id(1) == 0)
    def _():
        h_ref[...] = (x_ref[...] * (1.0 + sc_ref[0]) + sh_ref[0]).astype(BF)

    q = jnp.dot(h_ref[...], w_ref[...], preferred_element_type=F32).astype(BF)
    s_ref[0] = lax.dot_general(k_ref[0], q, (((1,), (1,)), ((), ())),
                               preferred_element_type=F32)


def _route_kernel(s_ref, st_ref, t2_scr, cand_scr):
    n_top = PEER_TOPK + 1
    neg = -jnp.inf
    tl = s_ref.shape[2]

    def top(x):
        out = []
        for _ in range(n_top):
            m = jnp.max(x, axis=0, keepdims=True)
            out.append(m)
            x = jnp.where(x == m, neg, x)
        return out

    t1 = top(s_ref[0])
    t2 = top(s_ref[1])
    for k in range(n_top):
        t2_scr[k:k + 1, :] = t2[k]
    cand_scr[...] = jnp.full(cand_scr.shape, neg, F32)
    off = 0
    for a in range(n_top):
        nb = n_top // (a + 1)
        cand_scr[off:off + nb, :] = t2_scr[0:nb, :] + t1[a]
        off += nb
    cand = cand_scr[...]
    c = cand
    kth = None
    nxt = None
    for it in range(n_top):
        m = jnp.max(c, axis=0, keepdims=True)
        if it == PEER_TOPK - 1:
            kth = m
        if it == PEER_TOPK:
            nxt = m
        c = jnp.where(c == m, neg, c)
    tau = 0.5 * (kth + nxt)
    top_sum = t1[0] + t2[0]
    z = jnp.sum(jnp.where(cand > tau, jnp.exp(cand - top_sum), 0.0), axis=0, keepdims=True)
    st_ref[0, 0:1, :] = tau
    st_ref[0, 1:2, :] = t1[0]
    st_ref[0, 2:3, :] = t2[0] + jnp.log(z)
    st_ref[0, 3:, :] = jnp.zeros((SUBLANES - 3, tl), F32)


def _n_cand(n_top):
    return sum(n_top // (a + 1) for a in range(n_top))


def _gelu_exact(x):
    return 0.5 * x * (1.0 + lax.erf(x * (1.0 / math.sqrt(2.0))))


def _peer_kernel(h_ref, s1_ref, s2_ref, st_ref, u_ref, v_ref, o_ref, e2_scr, a_scr, w_scr,
                 *, heads, keys):
    e = pl.program_id(1)
    nt = h_ref.shape[0]
    et = u_ref.shape[0]

    @pl.when(e == 0)
    def _():
        for h in range(heads):
            e2_scr[h] = jnp.exp(s2_ref[h] - st_ref[h, 2:3, :])
        o_ref[...] = jnp.zeros(o_ref.shape, F32)

    a_scr[...] = lax.dot_general(u_ref[...], h_ref[...], (((1,), (1,)), ((), ())),
                                 preferred_element_type=F32)
    rows = 64
    for ii in range(et // keys):
        for c in range(nt // LANES):
            cols = slice(c * LANES, (c + 1) * LANES)
            thr = []
            e1 = []
            for h in range(heads):
                s1 = s1_ref[ii, h:h + 1, cols]
                thr.append(st_ref[h, 0:1, cols] - s1)
                e1.append(jnp.exp(s1 - st_ref[h, 1:2, cols]))
            for r in range(keys // rows):
                rs = slice(r * rows, (r + 1) * rows)
                w = jnp.zeros((rows, LANES), F32)
                for h in range(heads):
                    sel = s2_ref[h, rs, cols] >= thr[h]
                    w = w + jnp.where(sel, e2_scr[h, rs, cols] * e1[h], 0.0)
                ars = slice(ii * keys + r * rows, ii * keys + (r + 1) * rows)
                w_scr[ars, cols] = (w * _gelu_exact(a_scr[ars, cols])).astype(BF)

    w_all = w_scr[...]
    nc = min(512, o_ref.shape[1])
    for n in range(o_ref.shape[1] // nc):
        cs = slice(n * nc, (n + 1) * nc)
        o_ref[:, cs] += lax.dot_general(w_all, v_ref[:, cs], (((0,), (0,)), ((), ())),
                                        preferred_element_type=F32)


def _ln2_kernel(x_ref, ff_ref, g2_ref, lg_ref, lb_ref, o_ref, *, alpha):
    _ln_inplace(o_ref, x_ref, g2_ref, lg_ref, lb_ref, ff_ref, alpha)


def _mod_spec(D, which, row_fn):
    return pl.BlockSpec((1, 1, D), lambda i, *_: (row_fn(i) * 6 + which, 0, 0))


def _path(x, mod, row_of_seq, shared_mod, rope_tab, wts, alpha, cache=None):
    B, T, D = x.shape
    M = B * T
    x2 = x.reshape(M, D)
    heads = wts["heads"]
    qr, kvr = wts["qr"], wts["kvr"]
    CW = wts["cw"]
    MW = heads * V_DIM
    tm = min(TOKEN_TILE, M) if shared_mod else min(TOKEN_TILE, T)
    assert M % tm == 0 and (T % tm == 0 or tm % T == 0)
    assert rope_tab.shape[0] == max(T, tm)

    def row_fn(i):
        return row_of_seq((i * tm) // T)

    def rope_idx(i, *_):
        return (i % max(T // tm, 1), 0)

    mspec = functools.partial(_mod_spec, D, row_fn=row_fn)
    row_spec = pl.BlockSpec((tm, D), lambda i, *_: (i, 0))

    wa = wts["w_a"]
    NA = wa.shape[1]
    qcn, ckv, kpe, kpad = pl.pallas_call(
        functools.partial(_inproj_a_kernel, qr=qr, kvr=kvr),
        grid=(M // tm,),
        in_specs=[row_spec, mspec(1), mspec(0),
                  pl.BlockSpec((D, NA), lambda i: (0, 0)),
                  pl.BlockSpec((1, qr), lambda i: (0, 0)),
                  pl.BlockSpec((1, kvr), lambda i: (0, 0)),
                  pl.BlockSpec((tm, LANES), rope_idx)],
        out_specs=[pl.BlockSpec((tm, qr), lambda i: (i, 0)),
                   pl.BlockSpec((tm, kvr), lambda i: (i, 0)),
                   pl.BlockSpec((tm, ROPE_DIM), lambda i: (i, 0)),
                   pl.BlockSpec((tm, LANES), lambda i: (i, 0))],
        out_shape=[jax.ShapeDtypeStruct((M, qr), BF),
                   jax.ShapeDtypeStruct((M, kvr), F32),
                   jax.ShapeDtypeStruct((M, ROPE_DIM), F32),
                   jax.ShapeDtypeStruct((M, LANES), BF)],
        compiler_params=_cp(("arbitrary",), 56),
        name="inproj_a",
    )(x2, mod, mod, wa, wts["g_q"], wts["g_kv"], rope_tab)

    wu = wts["w_u"]
    tn = min(512, CW)
    y = pl.pallas_call(
        _inproj_u_kernel,
        grid=(M // tm, CW // tn),
        in_specs=[row_spec, mspec(1), mspec(0),
                  pl.BlockSpec((D, tn), lambda i, j: (0, j)),
                  pl.BlockSpec((D, tn), lambda i, j: (0, j + CW // tn))],
        out_specs=pl.BlockSpec((tm, tn), lambda i, j: (i, j)),
        out_shape=jax.ShapeDtypeStruct((M, CW), F32),
        scratch_shapes=[pltpu.VMEM((tm, D), BF)],
        compiler_params=_cp(("arbitrary", "arbitrary"), 48),
        name="inproj_u",
    )(x2, mod, mod, wu, wu)

    wq = wts["w_q"]
    q = pl.pallas_call(
        functools.partial(_qproj_kernel, heads=heads),
        grid=(M // tm,),
        in_specs=[pl.BlockSpec((tm, qr), lambda i: (i, 0)),
                  pl.BlockSpec(wq.shape, lambda i: (0, 0)),
                  pl.BlockSpec((tm, LANES), rope_idx)],
        out_specs=pl.BlockSpec((tm, heads * HEAD_Q), lambda i: (i, 0)),
        out_shape=jax.ShapeDtypeStruct((M, heads * HEAD_Q), BF),
        compiler_params=_cp(("arbitrary",), 48),
        name="qproj",
    )(qcn, wq, rope_tab)

    if cache is not None:
        cache_ckv, cache_kpe = cache
        P = cache_ckv.shape[1]
        ckv_all = jnp.concatenate([cache_ckv, ckv.reshape(B, T, kvr)], axis=1)
        kpad_all = jnp.concatenate(
            [jnp.pad(cache_kpe, ((0, 0), (0, 0), (0, LANES - ROPE_DIM))).astype(BF),
             kpad.reshape(B, T, LANES)], axis=1)
    else:
        P = 0
        ckv_all = ckv.reshape(B, T, kvr)
        kpad_all = kpad.reshape(B, T, LANES)
    S = P + T
    MS = B * S
    tk = math.gcd(MS, TOKEN_TILE)
    wkv = wts["w_kv"]
    kv = pl.pallas_call(
        functools.partial(_kvproj_kernel, heads=heads),
        grid=(MS // tk,),
        in_specs=[pl.BlockSpec((tk, kvr), lambda i: (i, 0)),
                  pl.BlockSpec(wkv.shape, lambda i: (0, 0))],
        out_specs=pl.BlockSpec((tk, heads * HEAD_KV), lambda i: (i, 0)),
        out_shape=jax.ShapeDtypeStruct((MS, heads * HEAD_KV), BF),
        compiler_params=_cp(("arbitrary",), 40),
        name="kvproj",
    )(ckv_all.reshape(MS, kvr), wkv)

    tq = min(ATTN_Q_TILE, T)
    attn = pl.pallas_call(
        functools.partial(_attn_kernel, heads=heads, scale=(NOPE_DIM + ROPE_DIM) ** -0.5),
        grid=(B, T // tq),
        in_specs=[pl.BlockSpec((1, tq, heads * HEAD_Q), lambda b, t: (b, t, 0)),
                  pl.BlockSpec((1, S, heads * HEAD_KV), lambda b, t: (b, 0, 0)),
                  pl.BlockSpec((1, S, LANES), lambda b, t: (b, 0, 0))],
        out_specs=pl.BlockSpec((1, tq, MW), lambda b, t: (b, t, 0)),
        out_shape=jax.ShapeDtypeStruct((B, T, MW), BF),
        compiler_params=_cp(("arbitrary", "arbitrary"), 48),
        name="attn",
    )(q.reshape(B, T, heads * HEAD_Q), kv.reshape(B, S, heads * HEAD_KV), kpad_all)

    tt = min(CONV_T_TILE, T)
    taps = wts["w_dw"].shape[0]
    conv = pl.pallas_call(
        functools.partial(_conv_kernel, tt=tt, taps=taps),
        grid=(B, T // tt),
        in_specs=[pl.BlockSpec((1, T, CW), lambda b, t: (b, 0, 0)),
                  pl.BlockSpec((taps, CW), lambda b, t: (0, 0)),
                  pl.BlockSpec((1, CW), lambda b, t: (0, 0)),
                  pl.BlockSpec((1, CW), lambda b, t: (0, 0)),
                  pl.BlockSpec((1, CW), lambda b, t: (0, 0))],
        out_specs=pl.BlockSpec((1, tt, CW), lambda b, t: (b, t, 0)),
        out_shape=jax.ShapeDtypeStruct((B, T, CW), BF),
        scratch_shapes=[pltpu.VMEM((tt + 2 * CONV_HALO, CW), F32), pltpu.VMEM((tt, CW), F32)],
        compiler_params=_cp(("arbitrary", "arbitrary"), 40),
        name="conv",
    )(y.reshape(B, T, CW), wts["w_dw"], wts["b_dw"], wts["g_cn"], wts["b_cn"])

    wo = wts["w_out"]
    tkk = min(512, MW)
    na = MW // tkk
    nk = na + CW // tkk
    vec_spec = pl.BlockSpec((1, D), lambda i, *_: (0, 0))
    x1 = pl.pallas_call(
        functools.partial(_outproj_kernel, na=na, nk=nk, alpha=alpha),
        grid=(M // tm, nk),
        in_specs=[pl.BlockSpec((tm, tkk), lambda i, k: (i, jnp.minimum(k, na - 1))),
                  pl.BlockSpec((tm, tkk), lambda i, k: (i, jnp.maximum(k - na, 0))),
                  pl.BlockSpec((tkk, D), lambda i, k: (k, 0)),
                  row_spec, mspec(2), vec_spec, vec_spec],
        out_specs=row_spec,
        out_shape=jax.ShapeDtypeStruct((M, D), F32),
        compiler_params=_cp(("arbitrary", "arbitrary"), 56),
        name="outproj",
    )(attn.reshape(M, MW), conv.reshape(M, CW), wo, x2, mod, wts["ln1_g"], wts["ln1_b"])

    wpq = wts["w_pq"]
    keys_bf = wts["sub_keys"]
    nhp, nkeys, half = keys_bf.shape
    ph = nhp // 2
    h2, sT = pl.pallas_call(
        _peer_q_kernel,
        grid=(M // tm, nhp),
        in_specs=[row_spec, mspec(4), mspec(3),
                  pl.BlockSpec((D, half), lambda i, j: (0, j)),
                  pl.BlockSpec((1, nkeys, half), lambda i, j: (j, 0, 0))],
        out_specs=[pl.BlockSpec((tm, D), lambda i, j: (i, 0)),
                   pl.BlockSpec((1, nkeys, tm), lambda i, j: (j, 0, i))],
        out_shape=[jax.ShapeDtypeStruct((M, D), BF),
                   jax.ShapeDtypeStruct((nhp, nkeys, M), F32)],
        compiler_params=_cp(("arbitrary", "arbitrary"), 48),
        name="peer_q",
    )(x1, mod, mod, wpq, keys_bf)

    tl = min(256, M)
    stats = pl.pallas_call(
        _route_kernel,
        grid=(M // tl, ph),
        in_specs=[pl.BlockSpec((2, nkeys, tl), lambda i, h: (h, 0, i))],
        out_specs=pl.BlockSpec((1, SUBLANES, tl), lambda i, h: (h, 0, i)),
        out_shape=jax.ShapeDtypeStruct((ph, SUBLANES, M), F32),
        scratch_shapes=[pltpu.VMEM((24, tl), F32),
                        pltpu.VMEM((-(-_n_cand(PEER_TOPK + 1) // SUBLANES) * SUBLANES, tl), F32)],
        compiler_params=_cp(("arbitrary", "arbitrary"), 32),
        name="peer_route",
    )(sT)

    pu, pv = wts["peer_u"], wts["peer_v"]
    NE = pu.shape[0]
    nt = min(PEER_TOKEN_TILE, M)
    et = min(PEER_EXPERT_TILE, NE)
    assert et % nkeys == 0 and NE == nkeys * nkeys
    s4 = sT.reshape(ph, 2, nkeys, M)
    s1_rows = jnp.transpose(s4[:, 0], (1, 0, 2))
    ff = pl.pallas_call(
        functools.partial(_peer_kernel, heads=ph, keys=nkeys),
        grid=(M // nt, NE // et),
        in_specs=[pl.BlockSpec((nt, D), lambda i, e: (i, 0)),
                  pl.BlockSpec((et // nkeys, ph, nt), lambda i, e: (e, 0, i)),
                  pl.BlockSpec((ph, None, nkeys, nt), lambda i, e: (0, 1, 0, i)),
                  pl.BlockSpec((ph, SUBLANES, nt), lambda i, e: (0, 0, i)),
                  pl.BlockSpec((et, D), lambda i, e: (e, 0)),
                  pl.BlockSpec((et, D), lambda i, e: (e, 0))],
        out_specs=pl.BlockSpec((nt, D), lambda i, e: (i, 0)),
        out_shape=jax.ShapeDtypeStruct((M, D), F32),
        scratch_shapes=[pltpu.VMEM((ph, nkeys, nt), F32),
                        pltpu.VMEM((et, nt), F32),
                        pltpu.VMEM((et, nt), BF)],
        compiler_params=_cp(("arbitrary", "arbitrary"), 56),
        name="peer_main",
    )(h2, s1_rows, s4, stats, pu, pv)

    t2 = min(256, T)

    def row_fn2(i):
        return row_of_seq((i * t2) // T)

    row2 = pl.BlockSpec((t2, D), lambda i: (i, 0))
    y_out = pl.pallas_call(
        functools.partial(_ln2_kernel, alpha=alpha),
        grid=(M // t2,),
        in_specs=[row2, row2, _mod_spec(D, 5, row_fn2),
                  pl.BlockSpec((1, D), lambda i: (0, 0)), pl.BlockSpec((1, D), lambda i: (0, 0))],
        out_specs=row2,
        out_shape=jax.ShapeDtypeStruct((M, D), F32),
        compiler_params=_cp(("arbitrary",), 40),
        name="ln2",
    )(x1, ff, mod, wts["ln2_g"], wts["ln2_b"])

    return y_out.reshape(B, T, D), ckv.reshape(B, T, kvr), kpe.reshape(B, T, ROPE_DIM)


def _rope_table(n_tokens):
    rows = n_tokens // GRID_W
    row = jnp.repeat(jnp.arange(rows, dtype=F32), GRID_W)
    col = jnp.tile(jnp.arange(GRID_W, dtype=F32), rows)
    n_freq = ROPE_DIM // 4
    inv = ROPE_BASE ** (-jnp.arange(n_freq, dtype=F32) / n_freq)
    ang_r = row[:, None] * inv
    ang_c = col[:, None] * inv
    ang = jnp.concatenate([ang_r, ang_r, ang_c, ang_c], -1)
    return jnp.concatenate([jnp.cos(ang), jnp.sin(ang)], -1)


def _prep_weights(l, w_in, g_q, w_uq, g_kv, w_ukv, w_dw, b_dw, g_cn, b_cn, w_out,
                  ln1_g, ln1_b, w_pq, sub_keys, peer_u, peer_v, ln2_g, ln2_b):
    qr = g_q.shape[-1]
    kvr = g_kv.shape[-1]
    o3 = qr + kvr + ROPE_DIM
    heads = w_uq.shape[-1] // (NOPE_DIM + ROPE_DIM)
    wi = w_in[l]
    w_a = jnp.concatenate([wi[:, :o3], _rotate_half_axial(wi[:, qr + kvr:o3])], axis=1).astype(BF)
    w_u = wi[:, o3:].astype(BF)
    wq3 = w_uq[l].reshape(qr, heads, NOPE_DIM + ROPE_DIM)
    w_q = jnp.concatenate([wq3, _rotate_half_axial(wq3[..., NOPE_DIM:])], axis=-1)
    w_q = w_q.reshape(qr, heads * HEAD_Q).astype(BF)
    ph, two, nkeys, half = sub_keys.shape[1:]
    return dict(
        heads=heads, qr=qr, kvr=kvr, cw=w_dw.shape[-1],
        w_a=w_a, w_u=w_u, w_q=w_q, w_kv=w_ukv[l].astype(BF),
        g_q=g_q[l][None], g_kv=g_kv[l][None],
        w_dw=w_dw[l], b_dw=b_dw[l][None], g_cn=g_cn[l][None], b_cn=b_cn[l][None],
        w_out=w_out[l].astype(BF), ln1_g=ln1_g[l][None], ln1_b=ln1_b[l][None],
        w_pq=w_pq[l].astype(BF), sub_keys=sub_keys[l].reshape(ph * two, nkeys, half).astype(BF),
        peer_u=peer_u[l].astype(BF), peer_v=peer_v[l].astype(BF),
        ln2_g=ln2_g[l][None], ln2_b=ln2_b[l][None],
    )


def kernel(x_prompt, x_sample, cache_ckv, cache_kpe, c, c_ctx, w_ada, b_ada, w_in, g_q, w_uq,
           g_kv, w_ukv, w_dw, b_dw, g_cn, b_cn, w_out, ln1_g, ln1_b, w_pq, sub_keys, peer_u,
           peer_v, ln2_g, ln2_b):
    depth = w_ada.shape[0]
    alpha = (2 * depth) ** 0.25
    B, T, D = x_prompt.shape
    Bd, Td, _ = x_sample.shape
    n_rows = -(-(1 + Bd) // SUBLANES) * SUBLANES
    cond = jnp.concatenate([c_ctx[None, :], c, jnp.zeros((n_rows - 1 - Bd, D), F32)], axis=0)

    n_tab = max(T, min(TOKEN_TILE, B * T))
    ones_tab = jnp.concatenate(
        [jnp.ones((n_tab, ROPE_DIM), F32), jnp.zeros((n_tab, ROPE_DIM), F32)], -1)
    rope_tab = _rope_table(Td)

    xp, xs = x_prompt, x_sample
    ckv_layers, kpe_layers = [], []
    for l in range(depth):
        wts = _prep_weights(l, w_in, g_q, w_uq, g_kv, w_ukv, w_dw, b_dw, g_cn, b_cn, w_out,
                            ln1_g, ln1_b, w_pq, sub_keys, peer_u, peer_v, ln2_g, ln2_b)
        mod = _adaln(cond, w_ada[l], b_ada[l]).reshape(n_rows * 6, 1, D)
        xp, ckv, kpe = _path(xp, mod, lambda b: 0, True, ones_tab, wts, alpha)
        ckv_layers.append(ckv)
        kpe_layers.append(kpe)
        xs, _, _ = _path(xs, mod, lambda b: 1 + b, False, rope_tab, wts, alpha,
                         cache=(cache_ckv[:, l], cache_kpe[:, l]))
    return (xp, xs, jnp.stack(ckv_layers, axis=1), jnp.stack(kpe_layers, axis=1))
```

```python
import functools
import math

import jax
import jax.numpy as jnp
from jax import lax
from jax.experimental import pallas as pl
from jax.experimental.pallas import tpu as pltpu

F32 = jnp.float32
BF = jnp.bfloat16

NOPE_DIM = 128
ROPE_DIM = 64
V_DIM = 128
HEAD_Q = NOPE_DIM + 2 * ROPE_DIM
HEAD_KV = NOPE_DIM + V_DIM
GRID_W = 64
ROPE_BASE = 10000.0
PEER_TOPK = 16
EPS = 1e-6

LANES = 128
SUBLANES = 8
VMEM_MIB = 1 << 20

TOKEN_TILE = 512
PEER_TOKEN_TILE = 512
PEER_EXPERT_TILE = 512
PEER_Q_GROUP = 4
DOT_HEAD_GROUP = 4
ATTN_Q_TILE = 256
CONV_T_TILE = 256
CONV_T_CHUNK = 32
CONV_C_CHUNK = 256
CONV_HALO = 16


def _cp(sem, vmem_mib):
    return pltpu.CompilerParams(dimension_semantics=sem, vmem_limit_bytes=vmem_mib * VMEM_MIB)


def _rotate_half_axial(x):
    q = x.shape[-1] // 4
    a, b, c, d = (x[..., i * q:(i + 1) * q] for i in range(4))
    return jnp.concatenate([-b, a, -d, c], -1)


def _layer_norm_rows(z, g, b):
    mu = jnp.mean(z, -1, keepdims=True)
    zc = z - mu
    var = jnp.mean(zc * zc, -1, keepdims=True)
    return zc * lax.rsqrt(var + EPS) * g + b


def _rms_norm_rows(z, g):
    return z * lax.rsqrt(jnp.mean(z * z, -1, keepdims=True) + EPS) * g


def _adaln_kernel(c_ref, w_ref, b_ref, o_ref):
    c = c_ref[...]
    s = (c * jax.nn.sigmoid(c)).astype(BF)
    o_ref[...] = jnp.dot(s, w_ref[...].astype(BF), preferred_element_type=F32) + b_ref[...]


def _adaln(cond, w, b):
    R, D = cond.shape
    N = w.shape[1]
    tn = min(512, N)
    return pl.pallas_call(
        _adaln_kernel,
        grid=(N // tn,),
        in_specs=[pl.BlockSpec((R, D), lambda j: (0, 0)),
                  pl.BlockSpec((D, tn), lambda j: (0, j)),
                  pl.BlockSpec((1, tn), lambda j: (0, j))],
        out_specs=pl.BlockSpec((R, tn), lambda j: (0, j)),
        out_shape=jax.ShapeDtypeStruct((R, N), F32),
        compiler_params=_cp(("arbitrary",), 40),
        name="adaln",
    )(cond, w, b.reshape(1, N))


def _inproj_a_kernel(x_ref, sc_ref, sh_ref, w_ref, gq_ref, gkv_ref, rope_ref,
                     qcn_ref, ckv_ref, kpe_ref, kpad_ref, *, qr, kvr):
    h = (x_ref[...] * (1.0 + sc_ref[0]) + sh_ref[0]).astype(BF)
    p = jnp.dot(h, w_ref[...], preferred_element_type=F32)
    qcn_ref[...] = _rms_norm_rows(p[:, :qr], gq_ref[...]).astype(BF)
    ckv_ref[...] = _rms_norm_rows(p[:, qr:qr + kvr], gkv_ref[...])
    kslab = p[:, qr + kvr:qr + kvr + 2 * ROPE_DIM]
    kpe_ref[...] = kslab[:, :ROPE_DIM]
    u = kslab * rope_ref[...]
    kr = u + pltpu.roll(u, ROPE_DIM, 1)
    lane = lax.broadcasted_iota(jnp.int32, kr.shape, 1)
    kpad_ref[...] = jnp.where(lane < ROPE_DIM, kr, 0.0).astype(BF)


def _inproj_u_kernel(x_ref, sc_ref, sh_ref, wa_ref, wg_ref, y_ref, h_scr):
    @pl.when(pl.program_id(1) == 0)
    def _():
        h_scr[...] = (x_ref[...] * (1.0 + sc_ref[0]) + sh_ref[0]).astype(BF)

    h = h_scr[...]
    a = jnp.dot(h, wa_ref[...], preferred_element_type=F32)
    g = jnp.dot(h, wg_ref[...], preferred_element_type=F32)
    y_ref[...] = a * jax.nn.sigmoid(g)


def _qproj_kernel(x_ref, w_ref, rope_ref, o_ref, *, heads):
    x = x_ref[...]
    rope = rope_ref[...]
    group = math.gcd(heads, DOT_HEAD_GROUP)
    for g in range(heads // group):
        qg = jnp.dot(x, w_ref[:, g * group * HEAD_Q:(g + 1) * group * HEAD_Q],
                     preferred_element_type=F32)
        for j in range(group):
            b = (g * group + j) * HEAD_Q
            q = qg[:, j * HEAD_Q:(j + 1) * HEAD_Q]
            o_ref[:, b:b + NOPE_DIM] = q[:, :NOPE_DIM].astype(BF)
            u = q[:, NOPE_DIM:] * rope
            o_ref[:, b + NOPE_DIM:b + HEAD_Q] = (u + pltpu.roll(u, ROPE_DIM, 1)).astype(BF)


def _kvproj_kernel(x_ref, w_ref, o_ref, *, heads):
    x = x_ref[...].astype(BF)
    width = math.gcd(heads, DOT_HEAD_GROUP) * HEAD_KV
    for g in range(heads * HEAD_KV // width):
        o_ref[:, g * width:(g + 1) * width] = jnp.dot(
            x, w_ref[:, g * width:(g + 1) * width], preferred_element_type=F32).astype(BF)


def _attn_kernel(q_ref, kv_ref, kpad_ref, o_ref, *, heads, scale):
    kpad = kpad_ref[0]
    for h in range(heads):
        qh = q_ref[0, :, h * HEAD_Q:(h + 1) * HEAD_Q]
        kh = jnp.concatenate([kv_ref[0, :, h * HEAD_KV:h * HEAD_KV + NOPE_DIM], kpad], axis=1)
        s = lax.dot_general(qh, kh, (((1,), (1,)), ((), ())),
                            preferred_element_type=F32) * scale
        m = jnp.max(s, -1, keepdims=True)
        p = jnp.exp(s - m)
        l = jnp.sum(p, -1, keepdims=True)
        o = jnp.dot(p.astype(BF), kv_ref[0, :, h * HEAD_KV + NOPE_DIM:(h + 1) * HEAD_KV],
                    preferred_element_type=F32)
        o_ref[0, :, h * V_DIM:(h + 1) * V_DIM] = (o / l).astype(BF)


def _conv_kernel(y_ref, w_ref, bdw_ref, g_ref, b_ref, o_ref, pad_scr, z_scr, *, tt, taps):
    t = pl.program_id(1)
    nt = pl.num_programs(1)
    C = z_scr.shape[1]
    half = taps // 2
    pad_scr[CONV_HALO:CONV_HALO + tt, :] = y_ref[0, pl.ds(pl.multiple_of(t * tt, tt), tt), :]

    @pl.when(t == 0)
    def _():
        pad_scr[0:CONV_HALO, :] = jnp.zeros((CONV_HALO, C), F32)

    @pl.when(t > 0)
    def _():
        pad_scr[0:CONV_HALO, :] = y_ref[
            0, pl.ds(pl.multiple_of(t * tt - CONV_HALO, SUBLANES), CONV_HALO), :]

    @pl.when(t == nt - 1)
    def _():
        pad_scr[CONV_HALO + tt:, :] = jnp.zeros((CONV_HALO, C), F32)

    @pl.when(t < nt - 1)
    def _():
        pad_scr[CONV_HALO + tt:, :] = y_ref[
            0, pl.ds(pl.multiple_of((t + 1) * tt, tt), CONV_HALO), :]

    tc = CONV_T_CHUNK
    cc = min(CONV_C_CHUNK, C)
    base = CONV_HALO - half
    wlen = tc + 2 * CONV_HALO
    assert base >= 0 and base + taps - 1 + tc <= wlen

    def chan_body(ci, carry):
        c0 = pl.multiple_of(ci * cc, cc)
        for r in range(tt // tc):
            win = pad_scr[r * tc:r * tc + wlen, pl.ds(c0, cc)]
            acc = jnp.zeros((tc, cc), F32)
            for sh in range(SUBLANES):
                rolled = win if sh == 0 else pltpu.roll(win, wlen - sh, 0)
                for a in range(wlen // SUBLANES):
                    k = a * SUBLANES + sh - base
                    if 0 <= k < taps:
                        acc = acc + (rolled[a * SUBLANES:a * SUBLANES + tc, :]
                                     * w_ref[k:k + 1, pl.ds(c0, cc)])
            z_scr[r * tc:(r + 1) * tc, pl.ds(c0, cc)] = acc + bdw_ref[:, pl.ds(c0, cc)]
        return carry

    lax.fori_loop(0, C // cc, chan_body, 0)

    rows = 64
    for r in range(tt // rows):
        zn = _layer_norm_rows(z_scr[r * rows:(r + 1) * rows, :], g_ref[...], b_ref[...])
        o_ref[0, r * rows:(r + 1) * rows, :] = (zn * jax.nn.sigmoid(zn)).astype(BF)


def _ln_inplace(xo_ref, x_ref, gate_ref, lg_ref, lb_ref, ff_ref, alpha):
    rows = 64
    for r in range(xo_ref.shape[0] // rows):
        sl = slice(r * rows, (r + 1) * rows)
        z = alpha * x_ref[sl, :] + gate_ref[0] * ff_ref[sl, :]
        xo_ref[sl, :] = _layer_norm_rows(z, lg_ref[...], lb_ref[...])


def _outproj_kernel(a_ref, c_ref, w_ref, x_ref, g1_ref, lg_ref, lb_ref, xo_ref, *, na, nk, alpha):
    k = pl.program_id(1)

    @pl.when(k == 0)
    def _():
        xo_ref[...] = jnp.zeros(xo_ref.shape, F32)

    def accumulate(lhs_ref):
        lhs = lhs_ref[...]
        nc = min(512, xo_ref.shape[1])
        for n in range(xo_ref.shape[1] // nc):
            cs = slice(n * nc, (n + 1) * nc)
            xo_ref[:, cs] += jnp.dot(lhs, w_ref[:, cs], preferred_element_type=F32)

    @pl.when(k < na)
    def _():
        accumulate(a_ref)

    @pl.when(k >= na)
    def _():
        accumulate(c_ref)

    @pl.when(k == nk - 1)
    def _():
        _ln_inplace(xo_ref, x_ref, g1_ref, lg_ref, lb_ref, xo_ref, alpha)


def _peer_q_kernel(x_ref, sc_ref, sh_ref, w_ref, k_ref, h_ref, s_ref):
    @pl.when(pl.program_id(1) == 0)
    def _():
        h_ref[...] = (x_ref[...] * (1.0 + sc_ref[0]) + sh_ref[0]).astype(BF)

    half = k_ref.shape[2]
    q = jnp.dot(h_ref[...], w_ref[...], preferred_element_type=F32).astype(BF)
    for p in range(k_ref.shape[0]):
        s_ref[p] = lax.dot_general(k_ref[p], q[:, p * half:(p + 1) * half],
                                   (((1,), (1,)), ((), ())), preferred_element_type=F32)


def _route_kernel(s_ref, st_ref, e2_ref, t2_scr, cand_scr):
    n_top = PEER_TOPK + 1
    neg = -jnp.inf
    tl = s_ref.shape[2]

    def top(x):
        out = []
        for _ in range(n_top):
            m = jnp.max(x, axis=0, keepdims=True)
            out.append(m)
            x = jnp.where(x == m, neg, x)
        return out

    t1 = top(s_ref[0])
    t2 = top(s_ref[1])
    for k in range(n_top):
        t2_scr[k:k + 1, :] = t2[k]
    cand_scr[...] = jnp.full(cand_scr.shape, neg, F32)
    off = 0
    for a in range(n_top):
        nb = n_top // (a + 1)
        cand_scr[off:off + nb, :] = t2_scr[0:nb, :] + t1[a]
        off += nb
    cand = cand_scr[...]
    c = cand
    kth = None
    nxt = None
    for it in range(n_top):
        m = jnp.max(c, axis=0, keepdims=True)
        if it == PEER_TOPK - 1:
            kth = m
        if it == PEER_TOPK:
            nxt = m
        c = jnp.where(c == m, neg, c)
    tau = 0.5 * (kth + nxt)
    top_sum = t1[0] + t2[0]
    z = jnp.sum(jnp.where(cand > tau, jnp.exp(cand - top_sum), 0.0), axis=0, keepdims=True)
    st_ref[0, 0:1, :] = tau
    st_ref[0, 1:2, :] = t1[0]
    log_norm = t2[0] + jnp.log(z)
    st_ref[0, 2:3, :] = log_norm
    st_ref[0, 3:, :] = jnp.zeros((SUBLANES - 3, tl), F32)
    e2_ref[0] = jnp.exp(s_ref[1] - log_norm)


def _n_cand(n_top):
    return sum(n_top // (a + 1) for a in range(n_top))


def _gelu_exact(x):
    return 0.5 * x * (1.0 + lax.erf(x * (1.0 / math.sqrt(2.0))))


def _peer_kernel(h_ref, s1_ref, e2_ref, st_ref, u_ref, v_ref, o_ref, a_scr, w_scr, *, heads, keys):
    nt = h_ref.shape[0]
    et = u_ref.shape[0]
    n_halves = 2 if nt >= 2 * LANES else 1
    half = nt // n_halves

    @pl.when(pl.program_id(1) == 0)
    def _():
        o_ref[...] = jnp.zeros(o_ref.shape, F32)

    def weight_block(ii, c):
        cols = slice(c * LANES, (c + 1) * LANES)
        w = jnp.zeros((keys, LANES), F32)
        for h in range(heads):
            s1 = s1_ref[ii, h:h + 1, cols]
            e2_min = jnp.exp(st_ref[h, 0:1, cols] - s1 - st_ref[h, 2:3, cols])
            e1 = jnp.exp(s1 - st_ref[h, 1:2, cols])
            e2 = e2_ref[h, :, cols]
            w = w + jnp.where(e2 >= e2_min, e2, 0.0) * e1
        ars = slice(ii * keys, (ii + 1) * keys)
        w_scr[cols, ars] = (w * _gelu_exact(a_scr[ars, cols])).T.astype(BF)

    def up_rows(k):
        rs = slice(k * half, (k + 1) * half)
        w_rows = w_scr[rs, :]
        nc = min(512, o_ref.shape[1])
        for n in range(o_ref.shape[1] // nc):
            cs = slice(n * nc, (n + 1) * nc)
            o_ref[rs, cs] += jnp.dot(w_rows, v_ref[:, cs], preferred_element_type=F32)

    def down_cols(k):
        cols = slice(k * half, (k + 1) * half)
        a_scr[:, cols] = lax.dot_general(u_ref[...], h_ref[cols, :], (((1,), (1,)), ((), ())),
                                         preferred_element_type=F32)

    for k in range(n_halves):
        down_cols(k)
    for k in range(n_halves):
        for ii in range(et // keys):
            for c in range(k * half // LANES, (k + 1) * half // LANES):
                weight_block(ii, c)
        up_rows(k)


def _ln2_kernel(x_ref, ff_ref, g2_ref, lg_ref, lb_ref, o_ref, *, alpha):
    _ln_inplace(o_ref, x_ref, g2_ref, lg_ref, lb_ref, ff_ref, alpha)


def _mod_spec(D, which, row_fn):
    return pl.BlockSpec((1, 1, D), lambda i, *_: (row_fn(i) * 6 + which, 0, 0))


def _path(x, mod, row_of_seq, shared_mod, rope_tab, wts, alpha, cache=None):
    B, T, D = x.shape
    M = B * T
    x2 = x.reshape(M, D)
    heads = wts["heads"]
    qr, kvr = wts["qr"], wts["kvr"]
    CW = wts["cw"]
    MW = heads * V_DIM
    tm = min(TOKEN_TILE, M) if shared_mod else min(TOKEN_TILE, T)
    assert M % tm == 0 and (T % tm == 0 or tm % T == 0)
    assert rope_tab.shape[0] == max(T, tm)

    def row_fn(i):
        return row_of_seq((i * tm) // T)

    def rope_idx(i, *_):
        return (i % max(T // tm, 1), 0)

    mspec = functools.partial(_mod_spec, D, row_fn=row_fn)
    row_spec = pl.BlockSpec((tm, D), lambda i, *_: (i, 0))

    wa = wts["w_a"]
    NA = wa.shape[1]
    qcn, ckv, kpe, kpad = pl.pallas_call(
        functools.partial(_inproj_a_kernel, qr=qr, kvr=kvr),
        grid=(M // tm,),
        in_specs=[row_spec, mspec(1), mspec(0),
                  pl.BlockSpec((D, NA), lambda i: (0, 0)),
                  pl.BlockSpec((1, qr), lambda i: (0, 0)),
                  pl.BlockSpec((1, kvr), lambda i: (0, 0)),
                  pl.BlockSpec((tm, LANES), rope_idx)],
        out_specs=[pl.BlockSpec((tm, qr), lambda i: (i, 0)),
                   pl.BlockSpec((tm, kvr), lambda i: (i, 0)),
                   pl.BlockSpec((tm, ROPE_DIM), lambda i: (i, 0)),
                   pl.BlockSpec((tm, LANES), lambda i: (i, 0))],
        out_shape=[jax.ShapeDtypeStruct((M, qr), BF),
                   jax.ShapeDtypeStruct((M, kvr), F32),
                   jax.ShapeDtypeStruct((M, ROPE_DIM), F32),
                   jax.ShapeDtypeStruct((M, LANES), BF)],
        compiler_params=_cp(("arbitrary",), 56),
        name="inproj_a",
    )(x2, mod, mod, wa, wts["g_q"], wts["g_kv"], rope_tab)

    wu = wts["w_u"]
    tn = min(512, CW)
    y = pl.pallas_call(
        _inproj_u_kernel,
        grid=(M // tm, CW // tn),
        in_specs=[row_spec, mspec(1), mspec(0),
                  pl.BlockSpec((D, tn), lambda i, j: (0, j)),
                  pl.BlockSpec((D, tn), lambda i, j: (0, j + CW // tn))],
        out_specs=pl.BlockSpec((tm, tn), lambda i, j: (i, j)),
        out_shape=jax.ShapeDtypeStruct((M, CW), F32),
        scratch_shapes=[pltpu.VMEM((tm, D), BF)],
        compiler_params=_cp(("arbitrary", "arbitrary"), 48),
        name="inproj_u",
    )(x2, mod, mod, wu, wu)

    wq = wts["w_q"]
    q = pl.pallas_call(
        functools.partial(_qproj_kernel, heads=heads),
        grid=(M // tm,),
        in_specs=[pl.BlockSpec((tm, qr), lambda i: (i, 0)),
                  pl.BlockSpec(wq.shape, lambda i: (0, 0)),
                  pl.BlockSpec((tm, LANES), rope_idx)],
        out_specs=pl.BlockSpec((tm, heads * HEAD_Q), lambda i: (i, 0)),
        out_shape=jax.ShapeDtypeStruct((M, heads * HEAD_Q), BF),
        compiler_params=_cp(("arbitrary",), 48),
        name="qproj",
    )(qcn, wq, rope_tab)

    if cache is not None:
        cache_ckv, cache_kpe = cache
        P = cache_ckv.shape[1]
        ckv_all = jnp.concatenate([cache_ckv, ckv.reshape(B, T, kvr)], axis=1)
        kpad_all = jnp.concatenate(
            [jnp.pad(cache_kpe, ((0, 0), (0, 0), (0, LANES - ROPE_DIM))).astype(BF),
             kpad.reshape(B, T, LANES)], axis=1)
    else:
        P = 0
        ckv_all = ckv.reshape(B, T, kvr)
        kpad_all = kpad.reshape(B, T, LANES)
    S = P + T
    MS = B * S
    tk = math.gcd(MS, TOKEN_TILE)
    wkv = wts["w_kv"]
    kv = pl.pallas_call(
        functools.partial(_kvproj_kernel, heads=heads),
        grid=(MS // tk,),
        in_specs=[pl.BlockSpec((tk, kvr), lambda i: (i, 0)),
                  pl.BlockSpec(wkv.shape, lambda i: (0, 0))],
        out_specs=pl.BlockSpec((tk, heads * HEAD_KV), lambda i: (i, 0)),
        out_shape=jax.ShapeDtypeStruct((MS, heads * HEAD_KV), BF),
        compiler_params=_cp(("arbitrary",), 40),
        name="kvproj",
    )(ckv_all.reshape(MS, kvr), wkv)

    tq = min(ATTN_Q_TILE, T)
    attn = pl.pallas_call(
        functools.partial(_attn_kernel, heads=heads, scale=(NOPE_DIM + ROPE_DIM) ** -0.5),
        grid=(B, T // tq),
        in_specs=[pl.BlockSpec((1, tq, heads * HEAD_Q), lambda b, t: (b, t, 0)),
                  pl.BlockSpec((1, S, heads * HEAD_KV), lambda b, t: (b, 0, 0)),
                  pl.BlockSpec((1, S, LANES), lambda b, t: (b, 0, 0))],
        out_specs=pl.BlockSpec((1, tq, MW), lambda b, t: (b, t, 0)),
        out_shape=jax.ShapeDtypeStruct((B, T, MW), BF),
        compiler_params=_cp(("arbitrary", "arbitrary"), 48),
        name="attn",
    )(q.reshape(B, T, heads * HEAD_Q), kv.reshape(B, S, heads * HEAD_KV), kpad_all)

    tt = min(CONV_T_TILE, T)
    taps = wts["w_dw"].shape[0]
    conv = pl.pallas_call(
        functools.partial(_conv_kernel, tt=tt, taps=taps),
        grid=(B, T // tt),
        in_specs=[pl.BlockSpec((1, T, CW), lambda b, t: (b, 0, 0)),
                  pl.BlockSpec((taps, CW), lambda b, t: (0, 0)),
                  pl.BlockSpec((1, CW), lambda b, t: (0, 0)),
                  pl.BlockSpec((1, CW), lambda b, t: (0, 0)),
                  pl.BlockSpec((1, CW), lambda b, t: (0, 0))],
        out_specs=pl.BlockSpec((1, tt, CW), lambda b, t: (b, t, 0)),
        out_shape=jax.ShapeDtypeStruct((B, T, CW), BF),
        scratch_shapes=[pltpu.VMEM((tt + 2 * CONV_HALO, CW), F32), pltpu.VMEM((tt, CW), F32)],
        compiler_params=_cp(("arbitrary", "arbitrary"), 40),
        name="conv",
    )(y.reshape(B, T, CW), wts["w_dw"], wts["b_dw"], wts["g_cn"], wts["b_cn"])

    wo = wts["w_out"]
    tkk = min(512, MW)
    na = MW // tkk
    nk = na + CW // tkk
    vec_spec = pl.BlockSpec((1, D), lambda i, *_: (0, 0))
    x1 = pl.pallas_call(
        functools.partial(_outproj_kernel, na=na, nk=nk, alpha=alpha),
        grid=(M // tm, nk),
        in_specs=[pl.BlockSpec((tm, tkk), lambda i, k: (i, jnp.minimum(k, na - 1))),
                  pl.BlockSpec((tm, tkk), lambda i, k: (i, jnp.maximum(k - na, 0))),
                  pl.BlockSpec((tkk, D), lambda i, k: (k, 0)),
                  row_spec, mspec(2), vec_spec, vec_spec],
        out_specs=row_spec,
        out_shape=jax.ShapeDtypeStruct((M, D), F32),
        compiler_params=_cp(("arbitrary", "arbitrary"), 56),
        name="outproj",
    )(attn.reshape(M, MW), conv.reshape(M, CW), wo, x2, mod, wts["ln1_g"], wts["ln1_b"])

    wpq = wts["w_pq"]
    keys_bf = wts["sub_keys"]
    nhp, nkeys, half = keys_bf.shape
    ph = nhp // 2
    gq = math.gcd(nhp, PEER_Q_GROUP)
    h2, sT = pl.pallas_call(
        _peer_q_kernel,
        grid=(M // tm, nhp // gq),
        in_specs=[row_spec, mspec(4), mspec(3),
                  pl.BlockSpec((D, gq * half), lambda i, j: (0, j)),
                  pl.BlockSpec((gq, nkeys, half), lambda i, j: (j, 0, 0))],
        out_specs=[pl.BlockSpec((tm, D), lambda i, j: (i, 0)),
                   pl.BlockSpec((gq, nkeys, tm), lambda i, j: (j, 0, i))],
        out_shape=[jax.ShapeDtypeStruct((M, D), BF),
                   jax.ShapeDtypeStruct((nhp, nkeys, M), F32)],
        compiler_params=_cp(("arbitrary", "arbitrary"), 48),
        name="peer_q",
    )(x1, mod, mod, wpq, keys_bf)

    tl = min(256, M)
    stats, e2 = pl.pallas_call(
        _route_kernel,
        grid=(M // tl, ph),
        in_specs=[pl.BlockSpec((2, nkeys, tl), lambda i, h: (h, 0, i))],
        out_specs=[pl.BlockSpec((1, SUBLANES, tl), lambda i, h: (h, 0, i)),
                   pl.BlockSpec((1, nkeys, tl), lambda i, h: (h, 0, i))],
        out_shape=[jax.ShapeDtypeStruct((ph, SUBLANES, M), F32),
                   jax.ShapeDtypeStruct((ph, nkeys, M), F32)],
        scratch_shapes=[pltpu.VMEM((24, tl), F32),
                        pltpu.VMEM((-(-_n_cand(PEER_TOPK + 1) // SUBLANES) * SUBLANES, tl), F32)],
        compiler_params=_cp(("arbitrary", "arbitrary"), 32),
        name="peer_route",
    )(sT)

    pu, pv = wts["peer_u"], wts["peer_v"]
    NE = pu.shape[0]
    nt = min(PEER_TOKEN_TILE, M)
    et = min(PEER_EXPERT_TILE, NE)
    assert et % nkeys == 0 and NE == nkeys * nkeys
    s1_rows = jnp.transpose(sT.reshape(ph, 2, nkeys, M)[:, 0], (1, 0, 2))
    ff = pl.pallas_call(
        functools.partial(_peer_kernel, heads=ph, keys=nkeys),
        grid=(M // nt, NE // et),
        in_specs=[pl.BlockSpec((nt, D), lambda i, e: (i, 0)),
                  pl.BlockSpec((et // nkeys, ph, nt), lambda i, e: (e, 0, i)),
                  pl.BlockSpec((ph, nkeys, nt), lambda i, e: (0, 0, i)),
                  pl.BlockSpec((ph, SUBLANES, nt), lambda i, e: (0, 0, i)),
                  pl.BlockSpec((et, D), lambda i, e: (e, 0)),
                  pl.BlockSpec((et, D), lambda i, e: (e, 0))],
        out_specs=pl.BlockSpec((nt, D), lambda i, e: (i, 0)),
        out_shape=jax.ShapeDtypeStruct((M, D), F32),
        scratch_shapes=[pltpu.VMEM((et, nt), F32),
                        pltpu.VMEM((nt, et), BF)],
        compiler_params=_cp(("arbitrary", "arbitrary"), 56),
        name="peer_main",
    )(h2, s1_rows, e2, stats, pu, pv)

    t2 = min(256, T)

    def row_fn2(i):
        return row_of_seq((i * t2) // T)

    row2 = pl.BlockSpec((t2, D), lambda i: (i, 0))
    y_out = pl.pallas_call(
        functools.partial(_ln2_kernel, alpha=alpha),
        grid=(M // t2,),
        in_specs=[row2, row2, _mod_spec(D, 5, row_fn2),
                  pl.BlockSpec((1, D), lambda i: (0, 0)), pl.BlockSpec((1, D), lambda i: (0, 0))],
        out_specs=row2,
        out_shape=jax.ShapeDtypeStruct((M, D), F32),
        compiler_params=_cp(("arbitrary",), 40),
        name="ln2",
    )(x1, ff, mod, wts["ln2_g"], wts["ln2_b"])

    return y_out.reshape(B, T, D), ckv.reshape(B, T, kvr), kpe.reshape(B, T, ROPE_DIM)


def _rope_table(n_tokens):
    rows = n_tokens // GRID_W
    row = jnp.repeat(jnp.arange(rows, dtype=F32), GRID_W)
    col = jnp.tile(jnp.arange(GRID_W, dtype=F32), rows)
    n_freq = ROPE_DIM // 4
    inv = ROPE_BASE ** (-jnp.arange(n_freq, dtype=F32) / n_freq)
    ang_r = row[:, None] * inv
    ang_c = col[:, None] * inv
    ang = jnp.concatenate([ang_r, ang_r, ang_c, ang_c], -1)
    return jnp.concatenate([jnp.cos(ang), jnp.sin(ang)], -1)


def _prep_weights(l, w_in, g_q, w_uq, g_kv, w_ukv, w_dw, b_dw, g_cn, b_cn, w_out,
                  ln1_g, ln1_b, w_pq, sub_keys, peer_u, peer_v, ln2_g, ln2_b):
    qr = g_q.shape[-1]
    kvr = g_kv.shape[-1]
    o3 = qr + kvr + ROPE_DIM
    heads = w_uq.shape[-1] // (NOPE_DIM + ROPE_DIM)
    wi = w_in[l]
    w_a = jnp.concatenate([wi[:, :o3], _rotate_half_axial(wi[:, qr + kvr:o3])], axis=1).astype(BF)
    w_u = wi[:, o3:].astype(BF)
    wq3 = w_uq[l].reshape(qr, heads, NOPE_DIM + ROPE_DIM)
    w_q = jnp.concatenate([wq3, _rotate_half_axial(wq3[..., NOPE_DIM:])], axis=-1)
    w_q = w_q.reshape(qr, heads * HEAD_Q).astype(BF)
    ph, two, nkeys, half = sub_keys.shape[1:]
    return dict(
        heads=heads, qr=qr, kvr=kvr, cw=w_dw.shape[-1],
        w_a=w_a, w_u=w_u, w_q=w_q, w_kv=w_ukv[l].astype(BF),
        g_q=g_q[l][None], g_kv=g_kv[l][None],
        w_dw=w_dw[l], b_dw=b_dw[l][None], g_cn=g_cn[l][None], b_cn=b_cn[l][None],
        w_out=w_out[l].astype(BF), ln1_g=ln1_g[l][None], ln1_b=ln1_b[l][None],
        w_pq=w_pq[l].astype(BF), sub_keys=sub_keys[l].reshape(ph * two, nkeys, half).astype(BF),
        peer_u=peer_u[l].astype(BF), peer_v=peer_v[l].astype(BF),
        ln2_g=ln2_g[l][None], ln2_b=ln2_b[l][None],
    )


def kernel(x_prompt, x_sample, cache_ckv, cache_kpe, c, c_ctx, w_ada, b_ada, w_in, g_q, w_uq,
           g_kv, w_ukv, w_dw, b_dw, g_cn, b_cn, w_out, ln1_g, ln1_b, w_pq, sub_keys, peer_u,
           peer_v, ln2_g, ln2_b):
    depth = w_ada.shape[0]
    alpha = (2 * depth) ** 0.25
    B, T, D = x_prompt.shape
    Bd, Td, _ = x_sample.shape
    n_rows = -(-(1 + Bd) // SUBLANES) * SUBLANES
    cond = jnp.concatenate([c_ctx[None, :], c, jnp.zeros((n_rows - 1 - Bd, D), F32)], axis=0)

    n_tab = max(T, min(TOKEN_TILE, B * T))
    ones_tab = jnp.concatenate(
        [jnp.ones((n_tab, ROPE_DIM), F32), jnp.zeros((n_tab, ROPE_DIM), F32)], -1)
    rope_tab = _rope_table(Td)

    xp, xs = x_prompt, x_sample
    ckv_layers, kpe_layers = [], []
    for l in range(depth):
        wts = _prep_weights(l, w_in, g_q, w_uq, g_kv, w_ukv, w_dw, b_dw, g_cn, b_cn, w_out,
                            ln1_g, ln1_b, w_pq, sub_keys, peer_u, peer_v, ln2_g, ln2_b)
        mod = _adaln(cond, w_ada[l], b_ada[l]).reshape(n_rows * 6, 1, D)
        xp, ckv, kpe = _path(xp, mod, lambda b: 0, True, ones_tab, wts, alpha)
        ckv_layers.append(ckv)
        kpe_layers.append(kpe)
        xs, _, _ = _path(xs, mod, lambda b: 1 + b, False, rope_tab, wts, alpha,
                         cache=(cache_ckv[:, l], cache_kpe[:, l]))
    return (xp, xs, jnp.stack(ckv_layers, axis=1), jnp.stack(kpe_layers, axis=1))
```

```python
import functools
import math

import jax
import jax.numpy as jnp
from jax import lax
from jax.experimental import pallas as pl
from jax.experimental.pallas import tpu as pltpu

F32 = jnp.float32
BF = jnp.bfloat16

NOPE_DIM = 128
ROPE_DIM = 64
V_DIM = 128
HEAD_Q = NOPE_DIM + 2 * ROPE_DIM
HEAD_KV = NOPE_DIM + V_DIM
GRID_W = 64
ROPE_BASE = 10000.0
PEER_TOPK = 16
EPS = 1e-6

LANES = 128
SUBLANES = 8
VMEM_MIB = 1 << 20

TOKEN_TILE = 512
PEER_TOKEN_TILE = 512
PEER_EXPERT_TILE = 1024
PEER_Q_GROUP = 4
DOT_HEAD_GROUP = 4
ATTN_Q_TILE = 256
CONV_T_TILE = 256
CONV_T_CHUNK = 32
CONV_C_CHUNK = 256
CONV_HALO = 16


def _cp(sem, vmem_mib):
    return pltpu.CompilerParams(dimension_semantics=sem, vmem_limit_bytes=vmem_mib * VMEM_MIB)


def _rotate_half_axial(x):
    q = x.shape[-1] // 4
    a, b, c, d = (x[..., i * q:(i + 1) * q] for i in range(4))
    return jnp.concatenate([-b, a, -d, c], -1)


def _layer_norm_rows(z, g, b):
    mu = jnp.mean(z, -1, keepdims=True)
    zc = z - mu
    var = jnp.mean(zc * zc, -1, keepdims=True)
    return zc * lax.rsqrt(var + EPS) * g + b


def _rms_norm_rows(z, g):
    return z * lax.rsqrt(jnp.mean(z * z, -1, keepdims=True) + EPS) * g


def _adaln_kernel(c_ref, w_ref, b_ref, o_ref):
    c = c_ref[...]
    s = (c * jax.nn.sigmoid(c)).astype(BF)
    o_ref[...] = jnp.dot(s, w_ref[...].astype(BF), preferred_element_type=F32) + b_ref[...]


def _adaln(cond, w, b):
    R, D = cond.shape
    N = w.shape[1]
    tn = min(512, N)
    return pl.pallas_call(
        _adaln_kernel,
        grid=(N // tn,),
        in_specs=[pl.BlockSpec((R, D), lambda j: (0, 0)),
                  pl.BlockSpec((D, tn), lambda j: (0, j)),
                  pl.BlockSpec((1, tn), lambda j: (0, j))],
        out_specs=pl.BlockSpec((R, tn), lambda j: (0, j)),
        out_shape=jax.ShapeDtypeStruct((R, N), F32),
        compiler_params=_cp(("arbitrary",), 40),
        name="adaln",
    )(cond, w, b.reshape(1, N))


def _inproj_a_kernel(x_ref, sc_ref, sh_ref, w_ref, gq_ref, gkv_ref, rope_ref,
                     qcn_ref, ckv_ref, kpe_ref, kpad_ref, *, qr, kvr):
    h = (x_ref[...] * (1.0 + sc_ref[0]) + sh_ref[0]).astype(BF)
    p = jnp.dot(h, w_ref[...], preferred_element_type=F32)
    qcn_ref[...] = _rms_norm_rows(p[:, :qr], gq_ref[...]).astype(BF)
    ckv_ref[...] = _rms_norm_rows(p[:, qr:qr + kvr], gkv_ref[...])
    kslab = p[:, qr + kvr:qr + kvr + 2 * ROPE_DIM]
    kpe_ref[...] = kslab[:, :ROPE_DIM]
    u = kslab * rope_ref[...]
    kr = u + pltpu.roll(u, ROPE_DIM, 1)
    lane = lax.broadcasted_iota(jnp.int32, kr.shape, 1)
    kpad_ref[...] = jnp.where(lane < ROPE_DIM, kr, 0.0).astype(BF)


def _inproj_u_kernel(x_ref, sc_ref, sh_ref, wa_ref, wg_ref, y_ref, h_scr):
    @pl.when(pl.program_id(1) == 0)
    def _():
        h_scr[...] = (x_ref[...] * (1.0 + sc_ref[0]) + sh_ref[0]).astype(BF)

    h = h_scr[...]
    a = jnp.dot(h, wa_ref[...], preferred_element_type=F32)
    g = jnp.dot(h, wg_ref[...], preferred_element_type=F32)
    y_ref[...] = a * jax.nn.sigmoid(g)


def _qproj_kernel(x_ref, w_ref, rope_ref, o_ref, *, heads):
    x = x_ref[...]
    rope = rope_ref[...]
    group = math.gcd(heads, DOT_HEAD_GROUP)
    for g in range(heads // group):
        qg = jnp.dot(x, w_ref[:, g * group * HEAD_Q:(g + 1) * group * HEAD_Q],
                     preferred_element_type=F32)
        for j in range(group):
            b = (g * group + j) * HEAD_Q
            q = qg[:, j * HEAD_Q:(j + 1) * HEAD_Q]
            o_ref[:, b:b + NOPE_DIM] = q[:, :NOPE_DIM].astype(BF)
            u = q[:, NOPE_DIM:] * rope
            o_ref[:, b + NOPE_DIM:b + HEAD_Q] = (u + pltpu.roll(u, ROPE_DIM, 1)).astype(BF)


def _kvproj_kernel(x_ref, w_ref, o_ref, *, heads):
    x = x_ref[...].astype(BF)
    width = math.gcd(heads, DOT_HEAD_GROUP) * HEAD_KV
    for g in range(heads * HEAD_KV // width):
        o_ref[:, g * width:(g + 1) * width] = jnp.dot(
            x, w_ref[:, g * width:(g + 1) * width], preferred_element_type=F32).astype(BF)


def _attn_kernel(q_ref, kv_ref, kpad_ref, o_ref, *, heads, scale):
    kpad = kpad_ref[0]
    for h in range(heads):
        qh = q_ref[0, :, h * HEAD_Q:(h + 1) * HEAD_Q]
        kh = jnp.concatenate([kv_ref[0, :, h * HEAD_KV:h * HEAD_KV + NOPE_DIM], kpad], axis=1)
        s = lax.dot_general(qh, kh, (((1,), (1,)), ((), ())),
                            preferred_element_type=F32) * scale
        m = jnp.max(s, -1, keepdims=True)
        p = jnp.exp(s - m)
        l = jnp.sum(p, -1, keepdims=True)
        o = jnp.dot(p.astype(BF), kv_ref[0, :, h * HEAD_KV + NOPE_DIM:(h + 1) * HEAD_KV],
                    preferred_element_type=F32)
        o_ref[0, :, h * V_DIM:(h + 1) * V_DIM] = (o / l).astype(BF)


def _conv_kernel(y_ref, w_ref, bdw_ref, g_ref, b_ref, o_ref, pad_scr, z_scr, *, tt, taps):
    t = pl.program_id(1)
    nt = pl.num_programs(1)
    C = z_scr.shape[1]
    half = taps // 2
    pad_scr[CONV_HALO:CONV_HALO + tt, :] = y_ref[0, pl.ds(pl.multiple_of(t * tt, tt), tt), :]

    @pl.when(t == 0)
    def _():
        pad_scr[0:CONV_HALO, :] = jnp.zeros((CONV_HALO, C), F32)

    @pl.when(t > 0)
    def _():
        pad_scr[0:CONV_HALO, :] = y_ref[
            0, pl.ds(pl.multiple_of(t * tt - CONV_HALO, SUBLANES), CONV_HALO), :]

    @pl.when(t == nt - 1)
    def _():
        pad_scr[CONV_HALO + tt:, :] = jnp.zeros((CONV_HALO, C), F32)

    @pl.when(t < nt - 1)
    def _():
        pad_scr[CONV_HALO + tt:, :] = y_ref[
            0, pl.ds(pl.multiple_of((t + 1) * tt, tt), CONV_HALO), :]

    tc = CONV_T_CHUNK
    cc = min(CONV_C_CHUNK, C)
    base = CONV_HALO - half
    wlen = tc + 2 * CONV_HALO
    assert base >= 0 and base + taps - 1 + tc <= wlen

    def chan_body(ci, carry):
        c0 = pl.multiple_of(ci * cc, cc)
        for r in range(tt // tc):
            win = pad_scr[r * tc:r * tc + wlen, pl.ds(c0, cc)]
            acc = jnp.zeros((tc, cc), F32)
            for sh in range(SUBLANES):
                rolled = win if sh == 0 else pltpu.roll(win, wlen - sh, 0)
                for a in range(wlen // SUBLANES):
                    k = a * SUBLANES + sh - base
                    if 0 <= k < taps:
                        acc = acc + (rolled[a * SUBLANES:a * SUBLANES + tc, :]
                                     * w_ref[k:k + 1, pl.ds(c0, cc)])
            z_scr[r * tc:(r + 1) * tc, pl.ds(c0, cc)] = acc + bdw_ref[:, pl.ds(c0, cc)]
        return carry

    lax.fori_loop(0, C // cc, chan_body, 0)

    rows = 64
    for r in range(tt // rows):
        zn = _layer_norm_rows(z_scr[r * rows:(r + 1) * rows, :], g_ref[...], b_ref[...])
        o_ref[0, r * rows:(r + 1) * rows, :] = (zn * jax.nn.sigmoid(zn)).astype(BF)


def _ln_inplace(xo_ref, x_ref, gate_ref, lg_ref, lb_ref, ff_ref, alpha):
    rows = 64
    for r in range(xo_ref.shape[0] // rows):
        sl = slice(r * rows, (r + 1) * rows)
        z = alpha * x_ref[sl, :] + gate_ref[0] * ff_ref[sl, :]
        xo_ref[sl, :] = _layer_norm_rows(z, lg_ref[...], lb_ref[...])


def _outproj_kernel(a_ref, c_ref, w_ref, x_ref, g1_ref, lg_ref, lb_ref, xo_ref, *, na, nk, alpha):
    k = pl.program_id(1)

    @pl.when(k == 0)
    def _():
        xo_ref[...] = jnp.zeros(xo_ref.shape, F32)

    def accumulate(lhs_ref):
        lhs = lhs_ref[...]
        nc = min(512, xo_ref.shape[1])
        for n in range(xo_ref.shape[1] // nc):
            cs = slice(n * nc, (n + 1) * nc)
            xo_ref[:, cs] += jnp.dot(lhs, w_ref[:, cs], preferred_element_type=F32)

    @pl.when(k < na)
    def _():
        accumulate(a_ref)

    @pl.when(k >= na)
    def _():
        accumulate(c_ref)

    @pl.when(k == nk - 1)
    def _():
        _ln_inplace(xo_ref, x_ref, g1_ref, lg_ref, lb_ref, xo_ref, alpha)


def _peer_q_kernel(x_ref, sc_ref, sh_ref, w_ref, k_ref, h_ref, s_ref):
    @pl.when(pl.program_id(1) == 0)
    def _():
        h_ref[...] = (x_ref[...] * (1.0 + sc_ref[0]) + sh_ref[0]).astype(BF)

    half = k_ref.shape[2]
    q = jnp.dot(h_ref[...], w_ref[...], preferred_element_type=F32).astype(BF)
    for p in range(k_ref.shape[0]):
        s_ref[p] = lax.dot_general(k_ref[p], q[:, p * half:(p + 1) * half],
                                   (((1,), (1,)), ((), ())), preferred_element_type=F32)


def _route_kernel(s_ref, st_ref, e2_ref, t2_scr, cand_scr):
    n_top = PEER_TOPK + 1
    neg = -jnp.inf
    tl = s_ref.shape[2]

    def top(x):
        out = []
        for _ in range(n_top):
            m = jnp.max(x, axis=0, keepdims=True)
            out.append(m)
            x = jnp.where(x == m, neg, x)
        return out

    t1 = top(s_ref[0])
    t2 = top(s_ref[1])
    for k in range(n_top):
        t2_scr[k:k + 1, :] = t2[k]
    cand_scr[...] = jnp.full(cand_scr.shape, neg, F32)
    off = 0
    for a in range(n_top):
        nb = n_top // (a + 1)
        cand_scr[off:off + nb, :] = t2_scr[0:nb, :] + t1[a]
        off += nb
    cand = cand_scr[...]
    c = cand
    kth = None
    nxt = None
    for it in range(n_top):
        m = jnp.max(c, axis=0, keepdims=True)
        if it == PEER_TOPK - 1:
            kth = m
        if it == PEER_TOPK:
            nxt = m
        c = jnp.where(c == m, neg, c)
    tau = 0.5 * (kth + nxt)
    top_sum = t1[0] + t2[0]
    z = jnp.sum(jnp.where(cand > tau, jnp.exp(cand - top_sum), 0.0), axis=0, keepdims=True)
    st_ref[0, 0:1, :] = tau
    st_ref[0, 1:2, :] = t1[0]
    log_norm = t2[0] + jnp.log(z)
    st_ref[0, 2:3, :] = log_norm
    st_ref[0, 3:, :] = jnp.zeros((SUBLANES - 3, tl), F32)
    e2_ref[0] = jnp.exp(s_ref[1] - log_norm)


def _n_cand(n_top):
    return sum(n_top // (a + 1) for a in range(n_top))


def _gelu_exact(x):
    return 0.5 * x * (1.0 + lax.erf(x * (1.0 / math.sqrt(2.0))))


def _peer_kernel(h_ref, s1_ref, e2_ref, st_ref, u_ref, v_ref, o_ref, a_scr, w_scr, *, heads, keys):
    nt = h_ref.shape[0]
    et = u_ref.shape[0]
    n_halves = 2 if nt >= 2 * LANES else 1
    half = nt // n_halves

    @pl.when(pl.program_id(1) == 0)
    def _():
        o_ref[...] = jnp.zeros(o_ref.shape, F32)

    def weight_block(ii, c):
        cols = slice(c * LANES, (c + 1) * LANES)
        w = jnp.zeros((keys, LANES), F32)
        for h in range(heads):
            s1 = s1_ref[h, ii:ii + 1, cols]
            e2_min = jnp.exp(st_ref[h, 0:1, cols] - s1 - st_ref[h, 2:3, cols])
            e1 = jnp.exp(s1 - st_ref[h, 1:2, cols])
            e2 = e2_ref[h, :, cols]
            w = w + jnp.where(e2 >= e2_min, e2, 0.0) * e1
        ars = slice(ii * keys, (ii + 1) * keys)
        w_scr[cols, ars] = (w * _gelu_exact(a_scr[ars, cols])).T.astype(BF)

    def up_rows(k):
        rs = slice(k * half, (k + 1) * half)
        w_rows = w_scr[rs, :]
        nc = min(512, o_ref.shape[1])
        for n in range(o_ref.shape[1] // nc):
            cs = slice(n * nc, (n + 1) * nc)
            o_ref[rs, cs] += jnp.dot(w_rows, v_ref[:, cs], preferred_element_type=F32)

    def down_cols(k):
        cols = slice(k * half, (k + 1) * half)
        a_scr[:, cols] = lax.dot_general(u_ref[...], h_ref[cols, :], (((1,), (1,)), ((), ())),
                                         preferred_element_type=F32)

    for k in range(n_halves):
        down_cols(k)
    for k in range(n_halves):
        for ii in range(et // keys):
            for c in range(k * half // LANES, (k + 1) * half // LANES):
                weight_block(ii, c)
        up_rows(k)


def _ln2_kernel(x_ref, ff_ref, g2_ref, lg_ref, lb_ref, o_ref, *, alpha):
    _ln_inplace(o_ref, x_ref, g2_ref, lg_ref, lb_ref, ff_ref, alpha)


def _mod_spec(D, which, row_fn):
    return pl.BlockSpec((1, 1, D), lambda i, *_: (row_fn(i) * 6 + which, 0, 0))


def _path(x, mod, row_of_seq, shared_mod, rope_tab, wts, alpha, cache=None):
    B, T, D = x.shape
    M = B * T
    x2 = x.reshape(M, D)
    heads = wts["heads"]
    qr, kvr = wts["qr"], wts["kvr"]
    CW = wts["cw"]
    MW = heads * V_DIM
    tm = min(TOKEN_TILE, M) if shared_mod else min(TOKEN_TILE, T)
    assert M % tm == 0 and (T % tm == 0 or tm % T == 0)
    assert rope_tab.shape[0] == max(T, tm)

    def row_fn(i):
        return row_of_seq((i * tm) // T)

    def rope_idx(i, *_):
        return (i % max(T // tm, 1), 0)

    mspec = functools.partial(_mod_spec, D, row_fn=row_fn)
    row_spec = pl.BlockSpec((tm, D), lambda i, *_: (i, 0))

    wa = wts["w_a"]
    NA = wa.shape[1]
    qcn, ckv, kpe, kpad = pl.pallas_call(
        functools.partial(_inproj_a_kernel, qr=qr, kvr=kvr),
        grid=(M // tm,),
        in_specs=[row_spec, mspec(1), mspec(0),
                  pl.BlockSpec((D, NA), lambda i: (0, 0)),
                  pl.BlockSpec((1, qr), lambda i: (0, 0)),
                  pl.BlockSpec((1, kvr), lambda i: (0, 0)),
                  pl.BlockSpec((tm, LANES), rope_idx)],
        out_specs=[pl.BlockSpec((tm, qr), lambda i: (i, 0)),
                   pl.BlockSpec((tm, kvr), lambda i: (i, 0)),
                   pl.BlockSpec((tm, ROPE_DIM), lambda i: (i, 0)),
                   pl.BlockSpec((tm, LANES), lambda i: (i, 0))],
        out_shape=[jax.ShapeDtypeStruct((M, qr), BF),
                   jax.ShapeDtypeStruct((M, kvr), F32),
                   jax.ShapeDtypeStruct((M, ROPE_DIM), F32),
                   jax.ShapeDtypeStruct((M, LANES), BF)],
        compiler_params=_cp(("arbitrary",), 56),
        name="inproj_a",
    )(x2, mod, mod, wa, wts["g_q"], wts["g_kv"], rope_tab)

    wu = wts["w_u"]
    tn = min(512, CW)
    y = pl.pallas_call(
        _inproj_u_kernel,
        grid=(M // tm, CW // tn),
        in_specs=[row_spec, mspec(1), mspec(0),
                  pl.BlockSpec((D, tn), lambda i, j: (0, j)),
                  pl.BlockSpec((D, tn), lambda i, j: (0, j + CW // tn))],
        out_specs=pl.BlockSpec((tm, tn), lambda i, j: (i, j)),
        out_shape=jax.ShapeDtypeStruct((M, CW), F32),
        scratch_shapes=[pltpu.VMEM((tm, D), BF)],
        compiler_params=_cp(("arbitrary", "arbitrary"), 48),
        name="inproj_u",
    )(x2, mod, mod, wu, wu)

    wq = wts["w_q"]
    q = pl.pallas_call(
        functools.partial(_qproj_kernel, heads=heads),
        grid=(M // tm,),
        in_specs=[pl.BlockSpec((tm, qr), lambda i: (i, 0)),
                  pl.BlockSpec(wq.shape, lambda i: (0, 0)),
                  pl.BlockSpec((tm, LANES), rope_idx)],
        out_specs=pl.BlockSpec((tm, heads * HEAD_Q), lambda i: (i, 0)),
        out_shape=jax.ShapeDtypeStruct((M, heads * HEAD_Q), BF),
        compiler_params=_cp(("arbitrary",), 48),
        name="qproj",
    )(qcn, wq, rope_tab)

    if cache is not None:
        cache_ckv, cache_kpe = cache
        P = cache_ckv.shape[1]
        ckv_all = jnp.concatenate([cache_ckv, ckv.reshape(B, T, kvr)], axis=1)
        kpad_all = jnp.concatenate(
            [jnp.pad(cache_kpe, ((0, 0), (0, 0), (0, LANES - ROPE_DIM))).astype(BF),
             kpad.reshape(B, T, LANES)], axis=1)
    else:
        P = 0
        ckv_all = ckv.reshape(B, T, kvr)
        kpad_all = kpad.reshape(B, T, LANES)
    S = P + T
    MS = B * S
    tk = math.gcd(MS, TOKEN_TILE)
    wkv = wts["w_kv"]
    kv = pl.pallas_call(
        functools.partial(_kvproj_kernel, heads=heads),
        grid=(MS // tk,),
        in_specs=[pl.BlockSpec((tk, kvr), lambda i: (i, 0)),
                  pl.BlockSpec(wkv.shape, lambda i: (0, 0))],
        out_specs=pl.BlockSpec((tk, heads * HEAD_KV), lambda i: (i, 0)),
        out_shape=jax.ShapeDtypeStruct((MS, heads * HEAD_KV), BF),
        compiler_params=_cp(("arbitrary",), 40),
        name="kvproj",
    )(ckv_all.reshape(MS, kvr), wkv)

    tq = min(ATTN_Q_TILE, T)
    attn = pl.pallas_call(
        functools.partial(_attn_kernel, heads=heads, scale=(NOPE_DIM + ROPE_DIM) ** -0.5),
        grid=(B, T // tq),
        in_specs=[pl.BlockSpec((1, tq, heads * HEAD_Q), lambda b, t: (b, t, 0)),
                  pl.BlockSpec((1, S, heads * HEAD_KV), lambda b, t: (b, 0, 0)),
                  pl.BlockSpec((1, S, LANES), lambda b, t: (b, 0, 0))],
        out_specs=pl.BlockSpec((1, tq, MW), lambda b, t: (b, t, 0)),
        out_shape=jax.ShapeDtypeStruct((B, T, MW), BF),
        compiler_params=_cp(("arbitrary", "arbitrary"), 48),
        name="attn",
    )(q.reshape(B, T, heads * HEAD_Q), kv.reshape(B, S, heads * HEAD_KV), kpad_all)

    tt = min(CONV_T_TILE, T)
    taps = wts["w_dw"].shape[0]
    conv = pl.pallas_call(
        functools.partial(_conv_kernel, tt=tt, taps=taps),
        grid=(B, T // tt),
        in_specs=[pl.BlockSpec((1, T, CW), lambda b, t: (b, 0, 0)),
                  pl.BlockSpec((taps, CW), lambda b, t: (0, 0)),
                  pl.BlockSpec((1, CW), lambda b, t: (0, 0)),
                  pl.BlockSpec((1, CW), lambda b, t: (0, 0)),
                  pl.BlockSpec((1, CW), lambda b, t: (0, 0))],
        out_specs=pl.BlockSpec((1, tt, CW), lambda b, t: (b, t, 0)),
        out_shape=jax.ShapeDtypeStruct((B, T, CW), BF),
        scratch_shapes=[pltpu.VMEM((tt + 2 * CONV_HALO, CW), F32), pltpu.VMEM((tt, CW), F32)],
        compiler_params=_cp(("arbitrary", "arbitrary"), 40),
        name="conv",
    )(y.reshape(B, T, CW), wts["w_dw"], wts["b_dw"], wts["g_cn"], wts["b_cn"])

    wo = wts["w_out"]
    tkk = min(512, MW)
    na = MW // tkk
    nk = na + CW // tkk
    vec_spec = pl.BlockSpec((1, D), lambda i, *_: (0, 0))
    x1 = pl.pallas_call(
        functools.partial(_outproj_kernel, na=na, nk=nk, alpha=alpha),
        grid=(M // tm, nk),
        in_specs=[pl.BlockSpec((tm, tkk), lambda i, k: (i, jnp.minimum(k, na - 1))),
                  pl.BlockSpec((tm, tkk), lambda i, k: (i, jnp.maximum(k - na, 0))),
                  pl.BlockSpec((tkk, D), lambda i, k: (k, 0)),
                  row_spec, mspec(2), vec_spec, vec_spec],
        out_specs=row_spec,
        out_shape=jax.ShapeDtypeStruct((M, D), F32),
        compiler_params=_cp(("arbitrary", "arbitrary"), 56),
        name="outproj",
    )(attn.reshape(M, MW), conv.reshape(M, CW), wo, x2, mod, wts["ln1_g"], wts["ln1_b"])

    wpq = wts["w_pq"]
    keys_bf = wts["sub_keys"]
    nhp, nkeys, half = keys_bf.shape
    ph = nhp // 2
    gq = math.gcd(nhp, PEER_Q_GROUP)
    h2, sT = pl.pallas_call(
        _peer_q_kernel,
        grid=(M // tm, nhp // gq),
        in_specs=[row_spec, mspec(4), mspec(3),
                  pl.BlockSpec((D, gq * half), lambda i, j: (0, j)),
                  pl.BlockSpec((gq, nkeys, half), lambda i, j: (j, 0, 0))],
        out_specs=[pl.BlockSpec((tm, D), lambda i, j: (i, 0)),
                   pl.BlockSpec((gq, nkeys, tm), lambda i, j: (j, 0, i))],
        out_shape=[jax.ShapeDtypeStruct((M, D), BF),
                   jax.ShapeDtypeStruct((nhp, nkeys, M), F32)],
        compiler_params=_cp(("arbitrary", "arbitrary"), 48),
        name="peer_q",
    )(x1, mod, mod, wpq, keys_bf)

    tl = min(256, M)
    stats, e2 = pl.pallas_call(
        _route_kernel,
        grid=(M // tl, ph),
        in_specs=[pl.BlockSpec((2, nkeys, tl), lambda i, h: (h, 0, i))],
        out_specs=[pl.BlockSpec((1, SUBLANES, tl), lambda i, h: (h, 0, i)),
                   pl.BlockSpec((1, nkeys, tl), lambda i, h: (h, 0, i))],
        out_shape=[jax.ShapeDtypeStruct((ph, SUBLANES, M), F32),
                   jax.ShapeDtypeStruct((ph, nkeys, M), F32)],
        scratch_shapes=[pltpu.VMEM((24, tl), F32),
                        pltpu.VMEM((-(-_n_cand(PEER_TOPK + 1) // SUBLANES) * SUBLANES, tl), F32)],
        compiler_params=_cp(("arbitrary", "arbitrary"), 32),
        name="peer_route",
    )(sT)

    pu, pv = wts["peer_u"], wts["peer_v"]
    NE = pu.shape[0]
    nt = min(PEER_TOKEN_TILE, M)
    et = min(PEER_EXPERT_TILE, NE)
    rows = et // nkeys
    assert et % nkeys == 0 and NE == nkeys * nkeys and rows % SUBLANES == 0
    ff = pl.pallas_call(
        functools.partial(_peer_kernel, heads=ph, keys=nkeys),
        grid=(M // nt, NE // et),
        in_specs=[pl.BlockSpec((nt, D), lambda i, e: (i, 0)),
                  pl.BlockSpec((ph, None, rows, nt), lambda i, e: (0, 0, e, i)),
                  pl.BlockSpec((ph, nkeys, nt), lambda i, e: (0, 0, i)),
                  pl.BlockSpec((ph, SUBLANES, nt), lambda i, e: (0, 0, i)),
                  pl.BlockSpec((et, D), lambda i, e: (e, 0)),
                  pl.BlockSpec((et, D), lambda i, e: (e, 0))],
        out_specs=pl.BlockSpec((nt, D), lambda i, e: (i, 0), pipeline_mode=pl.Buffered(1)),
        out_shape=jax.ShapeDtypeStruct((M, D), F32),
        scratch_shapes=[pltpu.VMEM((et, nt), F32),
                        pltpu.VMEM((nt, et), BF)],
        compiler_params=_cp(("arbitrary", "arbitrary"), 60),
        name="peer_main",
    )(h2, sT.reshape(ph, 2, nkeys, M), e2, stats, pu, pv)

    t2 = min(256, T)

    def row_fn2(i):
        return row_of_seq((i * t2) // T)

    row2 = pl.BlockSpec((t2, D), lambda i: (i, 0))
    y_out = pl.pallas_call(
        functools.partial(_ln2_kernel, alpha=alpha),
        grid=(M // t2,),
        in_specs=[row2, row2, _mod_spec(D, 5, row_fn2),
                  pl.BlockSpec((1, D), lambda i: (0, 0)), pl.BlockSpec((1, D), lambda i: (0, 0))],
        out_specs=row2,
        out_shape=jax.ShapeDtypeStruct((M, D), F32),
        compiler_params=_cp(("arbitrary",), 40),
        name="ln2",
    )(x1, ff, mod, wts["ln2_g"], wts["ln2_b"])

    return y_out.reshape(B, T, D), ckv.reshape(B, T, kvr), kpe.reshape(B, T, ROPE_DIM)


def _rope_table(n_tokens):
    rows = n_tokens // GRID_W
    row = jnp.repeat(jnp.arange(rows, dtype=F32), GRID_W)
    col = jnp.tile(jnp.arange(GRID_W, dtype=F32), rows)
    n_freq = ROPE_DIM // 4
    inv = ROPE_BASE ** (-jnp.arange(n_freq, dtype=F32) / n_freq)
    ang_r = row[:, None] * inv
    ang_c = col[:, None] * inv
    ang = jnp.concatenate([ang_r, ang_r, ang_c, ang_c], -1)
    return jnp.concatenate([jnp.cos(ang), jnp.sin(ang)], -1)


def _prep_weights(l, w_in, g_q, w_uq, g_kv, w_ukv, w_dw, b_dw, g_cn, b_cn, w_out,
                  ln1_g, ln1_b, w_pq, sub_keys, peer_u, peer_v, ln2_g, ln2_b):
    qr = g_q.shape[-1]
    kvr = g_kv.shape[-1]
    o3 = qr + kvr + ROPE_DIM
    heads = w_uq.shape[-1] // (NOPE_DIM + ROPE_DIM)
    wi = w_in[l]
    w_a = jnp.concatenate([wi[:, :o3], _rotate_half_axial(wi[:, qr + kvr:o3])], axis=1).astype(BF)
    w_u = wi[:, o3:].astype(BF)
    wq3 = w_uq[l].reshape(qr, heads, NOPE_DIM + ROPE_DIM)
    w_q = jnp.concatenate([wq3, _rotate_half_axial(wq3[..., NOPE_DIM:])], axis=-1)
    w_q = w_q.reshape(qr, heads * HEAD_Q).astype(BF)
    ph, two, nkeys, half = sub_keys.shape[1:]
    return dict(
        heads=heads, qr=qr, kvr=kvr, cw=w_dw.shape[-1],
        w_a=w_a, w_u=w_u, w_q=w_q, w_kv=w_ukv[l].astype(BF),
        g_q=g_q[l][None], g_kv=g_kv[l][None],
        w_dw=w_dw[l], b_dw=b_dw[l][None], g_cn=g_cn[l][None], b_cn=b_cn[l][None],
        w_out=w_out[l].astype(BF), ln1_g=ln1_g[l][None], ln1_b=ln1_b[l][None],
        w_pq=w_pq[l].astype(BF), sub_keys=sub_keys[l].reshape(ph * two, nkeys, half).astype(BF),
        peer_u=peer_u[l].astype(BF), peer_v=peer_v[l].astype(BF),
        ln2_g=ln2_g[l][None], ln2_b=ln2_b[l][None],
    )


def kernel(x_prompt, x_sample, cache_ckv, cache_kpe, c, c_ctx, w_ada, b_ada, w_in, g_q, w_uq,
           g_kv, w_ukv, w_dw, b_dw, g_cn, b_cn, w_out, ln1_g, ln1_b, w_pq, sub_keys, peer_u,
           peer_v, ln2_g, ln2_b):
    depth = w_ada.shape[0]
    alpha = (2 * depth) ** 0.25
    B, T, D = x_prompt.shape
    Bd, Td, _ = x_sample.shape
    n_rows = -(-(1 + Bd) // SUBLANES) * SUBLANES
    cond = jnp.concatenate([c_ctx[None, :], c, jnp.zeros((n_rows - 1 - Bd, D), F32)], axis=0)

    n_tab = max(T, min(TOKEN_TILE, B * T))
    ones_tab = jnp.concatenate(
        [jnp.ones((n_tab, ROPE_DIM), F32), jnp.zeros((n_tab, ROPE_DIM), F32)], -1)
    rope_tab = _rope_table(Td)

    xp, xs = x_prompt, x_sample
    ckv_layers, kpe_layers = [], []
    for l in range(depth):
        wts = _prep_weights(l, w_in, g_q, w_uq, g_kv, w_ukv, w_dw, b_dw, g_cn, b_cn, w_out,
                            ln1_g, ln1_b, w_pq, sub_keys, peer_u, peer_v, ln2_g, ln2_b)
        mod = _adaln(cond, w_ada[l], b_ada[l]).reshape(n_rows * 6, 1, D)
        xp, ckv, kpe = _path(xp, mod, lambda b: 0, True, ones_tab, wts, alpha)
        ckv_layers.append(ckv)
        kpe_layers.append(kpe)
        xs, _, _ = _path(xs, mod, lambda b: 1 + b, False, rope_tab, wts, alpha,
                         cache=(cache_ckv[:, l], cache_kpe[:, l]))
    return (xp, xs, jnp.stack(ckv_layers, axis=1), jnp.stack(kpe_layers, axis=1))
```

```python
import functools
import math

import jax
import jax.numpy as jnp
from jax import lax
from jax.experimental import pallas as pl
from jax.experimental.pallas import tpu as pltpu

F32 = jnp.float32
BF = jnp.bfloat16

NOPE_DIM = 128
ROPE_DIM = 64
V_DIM = 128
HEAD_Q = NOPE_DIM + 2 * ROPE_DIM
HEAD_KV = NOPE_DIM + V_DIM
GRID_W = 64
ROPE_BASE = 10000.0
PEER_TOPK = 16
EPS = 1e-6

LANES = 128
SUBLANES = 8
VMEM_MIB = 1 << 20

TOKEN_TILE = 512
PEER_TOKEN_TILE = 512
PEER_EXPERT_TILE = 1024
PEER_Q_GROUP = 4
DOT_HEAD_GROUP = 4
ATTN_Q_TILE = 256
CONV_T_TILE = 256
CONV_T_CHUNK = 32
CONV_C_CHUNK = 256
CONV_HALO = 16


def _cp(sem, vmem_mib):
    return pltpu.CompilerParams(dimension_semantics=sem, vmem_limit_bytes=vmem_mib * VMEM_MIB)


def _rope_lanes(x, tab):
    q = ROPE_DIM // 4
    lane = lax.broadcasted_iota(jnp.int32, x.shape, 1)
    x = jnp.where(lane < ROPE_DIM, x, 0.0)
    nxt = pltpu.roll(x, LANES - q, 1)
    prv = pltpu.roll(x, q, 1)
    rot = jnp.where((lane // q) % 2 == 0, -nxt, prv)
    return x * tab[:, :LANES] + rot * tab[:, LANES:]


def _layer_norm_rows(z, g, b):
    mu = jnp.mean(z, -1, keepdims=True)
    zc = z - mu
    var = jnp.mean(zc * zc, -1, keepdims=True)
    return zc * lax.rsqrt(var + EPS) * g + b


def _rms_norm_rows(z, g):
    return z * lax.rsqrt(jnp.mean(z * z, -1, keepdims=True) + EPS) * g


def _adaln_kernel(c_ref, w_ref, b_ref, o_ref):
    c = c_ref[...]
    s = (c * jax.nn.sigmoid(c)).astype(BF)
    o_ref[...] = jnp.dot(s, w_ref[...].astype(BF), preferred_element_type=F32) + b_ref[...]


def _adaln(cond, w, b):
    R, D = cond.shape
    N = w.shape[1]
    tn = min(512, N)
    return pl.pallas_call(
        _adaln_kernel,
        grid=(N // tn,),
        in_specs=[pl.BlockSpec((R, D), lambda j: (0, 0)),
                  pl.BlockSpec((D, tn), lambda j: (0, j)),
                  pl.BlockSpec((1, tn), lambda j: (0, j))],
        out_specs=pl.BlockSpec((R, tn), lambda j: (0, j)),
        out_shape=jax.ShapeDtypeStruct((R, N), F32),
        compiler_params=_cp(("arbitrary",), 40),
        name="adaln",
    )(cond, w, b.reshape(1, N))


def _inproj_a_kernel(x_ref, sc_ref, sh_ref, w_ref, gq_ref, gkv_ref, rope_ref,
                     qcn_ref, ckv_ref, kpe_ref, kpad_ref, *, qr, kvr):
    h = (x_ref[...] * (1.0 + sc_ref[0]) + sh_ref[0]).astype(BF)
    p = jnp.dot(h, w_ref[...], preferred_element_type=F32)
    qcn_ref[...] = _rms_norm_rows(p[:, :qr], gq_ref[...]).astype(BF)
    ckv_ref[...] = _rms_norm_rows(p[:, qr:qr + kvr], gkv_ref[...])
    kslab = p[:, qr + kvr:qr + kvr + LANES]
    kpe_ref[...] = kslab[:, :ROPE_DIM]
    kpad_ref[...] = _rope_lanes(kslab, rope_ref[...]).astype(BF)


def _inproj_u_kernel(x_ref, sc_ref, sh_ref, wa_ref, wg_ref, y_ref, h_scr):
    @pl.when(pl.program_id(1) == 0)
    def _():
        h_scr[...] = (x_ref[...] * (1.0 + sc_ref[0]) + sh_ref[0]).astype(BF)

    h = h_scr[...]
    a = jnp.dot(h, wa_ref[...], preferred_element_type=F32)
    g = jnp.dot(h, wg_ref[...], preferred_element_type=F32)
    y_ref[...] = a * jax.nn.sigmoid(g)


def _qproj_kernel(x_ref, w_ref, rope_ref, o_ref, *, heads):
    x = x_ref[...]
    rope = rope_ref[...]
    group = math.gcd(heads, DOT_HEAD_GROUP)
    for g in range(heads // group):
        qg = jnp.dot(x, w_ref[:, g * group * HEAD_Q:(g + 1) * group * HEAD_Q],
                     preferred_element_type=F32)
        for j in range(group):
            b = (g * group + j) * HEAD_Q
            q = qg[:, j * HEAD_Q:(j + 1) * HEAD_Q]
            o_ref[:, b:b + NOPE_DIM] = q[:, :NOPE_DIM].astype(BF)
            o_ref[:, b + NOPE_DIM:b + HEAD_Q] = _rope_lanes(q[:, NOPE_DIM:], rope).astype(BF)


def _kvproj_kernel(x_ref, w_ref, o_ref, *, heads):
    x = x_ref[...].astype(BF)
    width = math.gcd(heads, DOT_HEAD_GROUP) * HEAD_KV
    for g in range(heads * HEAD_KV // width):
        o_ref[:, g * width:(g + 1) * width] = jnp.dot(
            x, w_ref[:, g * width:(g + 1) * width], preferred_element_type=F32).astype(BF)


def _attn_kernel(q_ref, kv_ref, kpad_ref, o_ref, *, heads, scale):
    kpad = kpad_ref[0]
    for h in range(heads):
        qh = q_ref[0, :, h * HEAD_Q:(h + 1) * HEAD_Q]
        kh = jnp.concatenate([kv_ref[0, :, h * HEAD_KV:h * HEAD_KV + NOPE_DIM], kpad], axis=1)
        s = lax.dot_general(qh, kh, (((1,), (1,)), ((), ())),
                            preferred_element_type=F32) * scale
        m = jnp.max(s, -1, keepdims=True)
        p = jnp.exp(s - m)
        l = jnp.sum(p, -1, keepdims=True)
        o = jnp.dot(p.astype(BF), kv_ref[0, :, h * HEAD_KV + NOPE_DIM:(h + 1) * HEAD_KV],
                    preferred_element_type=F32)
        o_ref[0, :, h * V_DIM:(h + 1) * V_DIM] = (o / l).astype(BF)


def _conv_kernel(y_ref, w_ref, bdw_ref, g_ref, b_ref, o_ref, pad_scr, z_scr, *, tt, taps):
    t = pl.program_id(1)
    nt = pl.num_programs(1)
    C = z_scr.shape[1]
    half = taps // 2
    pad_scr[CONV_HALO:CONV_HALO + tt, :] = y_ref[0, pl.ds(pl.multiple_of(t * tt, tt), tt), :]

    @pl.when(t == 0)
    def _():
        pad_scr[0:CONV_HALO, :] = jnp.zeros((CONV_HALO, C), F32)

    @pl.when(t > 0)
    def _():
        pad_scr[0:CONV_HALO, :] = y_ref[
            0, pl.ds(pl.multiple_of(t * tt - CONV_HALO, SUBLANES), CONV_HALO), :]

    @pl.when(t == nt - 1)
    def _():
        pad_scr[CONV_HALO + tt:, :] = jnp.zeros((CONV_HALO, C), F32)

    @pl.when(t < nt - 1)
    def _():
        pad_scr[CONV_HALO + tt:, :] = y_ref[
            0, pl.ds(pl.multiple_of((t + 1) * tt, tt), CONV_HALO), :]

    tc = CONV_T_CHUNK
    cc = min(CONV_C_CHUNK, C)
    base = CONV_HALO - half
    wlen = tc + 2 * CONV_HALO
    assert base >= 0 and base + taps - 1 + tc <= wlen

    def chan_body(ci, carry):
        c0 = pl.multiple_of(ci * cc, cc)
        for r in range(tt // tc):
            win = pad_scr[r * tc:r * tc + wlen, pl.ds(c0, cc)]
            acc = jnp.zeros((tc, cc), F32)
            for sh in range(SUBLANES):
                rolled = win if sh == 0 else pltpu.roll(win, wlen - sh, 0)
                for a in range(wlen // SUBLANES):
                    k = a * SUBLANES + sh - base
                    if 0 <= k < taps:
                        acc = acc + (rolled[a * SUBLANES:a * SUBLANES + tc, :]
                                     * w_ref[k:k + 1, pl.ds(c0, cc)])
            z_scr[r * tc:(r + 1) * tc, pl.ds(c0, cc)] = acc + bdw_ref[:, pl.ds(c0, cc)]
        return carry

    lax.fori_loop(0, C // cc, chan_body, 0)

    rows = 64
    for r in range(tt // rows):
        zn = _layer_norm_rows(z_scr[r * rows:(r + 1) * rows, :], g_ref[...], b_ref[...])
        o_ref[0, r * rows:(r + 1) * rows, :] = (zn * jax.nn.sigmoid(zn)).astype(BF)


def _ln_inplace(xo_ref, x_ref, gate_ref, lg_ref, lb_ref, ff_ref, alpha):
    rows = 64
    for r in range(xo_ref.shape[0] // rows):
        sl = slice(r * rows, (r + 1) * rows)
        z = alpha * x_ref[sl, :] + gate_ref[0] * ff_ref[sl, :]
        xo_ref[sl, :] = _layer_norm_rows(z, lg_ref[...], lb_ref[...])


def _outproj_kernel(a_ref, c_ref, w_ref, x_ref, g1_ref, lg_ref, lb_ref, xo_ref, *, na, nk, alpha):
    k = pl.program_id(1)

    @pl.when(k == 0)
    def _():
        xo_ref[...] = jnp.zeros(xo_ref.shape, F32)

    def accumulate(lhs_ref):
        lhs = lhs_ref[...]
        nc = min(512, xo_ref.shape[1])
        for n in range(xo_ref.shape[1] // nc):
            cs = slice(n * nc, (n + 1) * nc)
            xo_ref[:, cs] += jnp.dot(lhs, w_ref[:, cs], preferred_element_type=F32)

    @pl.when(k < na)
    def _():
        accumulate(a_ref)

    @pl.when(k >= na)
    def _():
        accumulate(c_ref)

    @pl.when(k == nk - 1)
    def _():
        _ln_inplace(xo_ref, x_ref, g1_ref, lg_ref, lb_ref, xo_ref, alpha)


def _peer_q_kernel(x_ref, sc_ref, sh_ref, w_ref, k_ref, h_ref, s_ref):
    @pl.when(pl.program_id(1) == 0)
    def _():
        h_ref[...] = (x_ref[...] * (1.0 + sc_ref[0]) + sh_ref[0]).astype(BF)

    half = k_ref.shape[2]
    q = jnp.dot(h_ref[...], w_ref[...], preferred_element_type=F32).astype(BF)
    for p in range(k_ref.shape[0]):
        s_ref[p] = lax.dot_general(k_ref[p], q[:, p * half:(p + 1) * half],
                                   (((1,), (1,)), ((), ())), preferred_element_type=F32)


def _route_kernel(s_ref, st_ref, e2_ref, t2_scr, cand_scr):
    n_top = PEER_TOPK + 1
    neg = -jnp.inf
    tl = s_ref.shape[2]

    def top(p):
        n_grp = s_ref.shape[1] // SUBLANES
        col = [s_ref[p, g * SUBLANES:(g + 1) * SUBLANES, :] for g in range(n_grp)]
        for i, j in _sorting_network(n_grp):
            col[i], col[j] = jnp.maximum(col[i], col[j]), jnp.minimum(col[i], col[j])
        col.append(jnp.full(col[0].shape, neg, F32))
        out = []
        for it in range(n_top):
            m = jnp.max(col[0], axis=0, keepdims=True)
            out.append(m)
            took = col[0] == m
            for g in range(min(n_grp, n_top - 1 - it)):
                col[g] = jnp.where(took, col[g + 1], col[g])
        return out

    t1 = top(0)
    t2 = top(1)
    for k in range(n_top):
        t2_scr[k:k + 1, :] = t2[k]
    cand_scr[...] = jnp.full(cand_scr.shape, neg, F32)
    off = 0
    for a in range(n_top):
        nb = n_top // (a + 1)
        cand_scr[off:off + nb, :] = t2_scr[0:nb, :] + t1[a]
        off += nb
    cand = cand_scr[...]
    c = cand
    kth = None
    nxt = None
    for it in range(n_top):
        m = jnp.max(c, axis=0, keepdims=True)
        if it == PEER_TOPK - 1:
            kth = m
        if it == PEER_TOPK:
            nxt = m
        c = jnp.where(c == m, neg, c)
    tau = 0.5 * (kth + nxt)
    top_sum = t1[0] + t2[0]
    z = jnp.sum(jnp.where(cand > tau, jnp.exp(cand - top_sum), 0.0), axis=0, keepdims=True)
    st_ref[0, 0:1, :] = tau
    st_ref[0, 1:2, :] = t1[0]
    log_norm = t2[0] + jnp.log(z)
    st_ref[0, 2:3, :] = log_norm
    st_ref[0, 3:, :] = jnp.zeros((SUBLANES - 3, tl), F32)
    e2_ref[0] = jnp.exp(s_ref[1] - log_norm)


def _sorting_network(n):
    assert n & (n - 1) == 0

    def merge(lo, hi, r):
        step = r * 2
        if step < hi - lo:
            yield from merge(lo, hi, step)
            yield from merge(lo + r, hi, step)
            yield from ((i, i + r) for i in range(lo + r, hi - r, step))
        else:
            yield (lo, lo + r)

    def sort(lo, hi):
        if hi - lo >= 1:
            mid = lo + (hi - lo) // 2
            yield from sort(lo, mid)
            yield from sort(mid + 1, hi)
            yield from merge(lo, hi, 1)

    return list(sort(0, n - 1))


def _n_cand(n_top):
    return sum(n_top // (a + 1) for a in range(n_top))


def _gelu_exact(x):
    return 0.5 * x * (1.0 + lax.erf(x * (1.0 / math.sqrt(2.0))))


def _peer_kernel(h_ref, s1_ref, e2_ref, st_ref, u_ref, v_ref, o_ref, a_scr, w_scr, *, heads, keys):
    nt = h_ref.shape[0]
    et = u_ref.shape[0]
    n_halves = 2 if nt >= 2 * LANES else 1
    half = nt // n_halves

    @pl.when(pl.program_id(1) == 0)
    def _():
        o_ref[...] = jnp.zeros(o_ref.shape, F32)

    def weight_block(ii, c):
        cols = slice(c * LANES, (c + 1) * LANES)
        w = jnp.zeros((keys, LANES), F32)
        for h in range(heads):
            s1 = s1_ref[h, ii:ii + 1, cols]
            e2_min = jnp.exp(st_ref[h, 0:1, cols] - s1 - st_ref[h, 2:3, cols])
            e1 = jnp.exp(s1 - st_ref[h, 1:2, cols])
            e2 = e2_ref[h, :, cols]
            w = w + jnp.where(e2 >= e2_min, e2, 0.0) * e1
        ars = slice(ii * keys, (ii + 1) * keys)
        w_scr[cols, ars] = (w * _gelu_exact(a_scr[ars, cols])).T.astype(BF)

    def up_rows(k):
        rs = slice(k * half, (k + 1) * half)
        w_rows = w_scr[rs, :]
        nc = min(512, o_ref.shape[1])
        for n in range(o_ref.shape[1] // nc):
            cs = slice(n * nc, (n + 1) * nc)
            o_ref[rs, cs] += jnp.dot(w_rows, v_ref[:, cs], preferred_element_type=F32)

    def down_cols(k):
        cols = slice(k * half, (k + 1) * half)
        a_scr[:, cols] = lax.dot_general(u_ref[...], h_ref[cols, :], (((1,), (1,)), ((), ())),
                                         preferred_element_type=F32)

    for k in range(n_halves):
        down_cols(k)
    for k in range(n_halves):
        for ii in range(et // keys):
            for c in range(k * half // LANES, (k + 1) * half // LANES):
                weight_block(ii, c)
        up_rows(k)


def _ln2_kernel(x_ref, ff_ref, g2_ref, lg_ref, lb_ref, o_ref, *, alpha):
    _ln_inplace(o_ref, x_ref, g2_ref, lg_ref, lb_ref, ff_ref, alpha)


def _mod_spec(D, which, row_fn):
    return pl.BlockSpec((1, 1, D), lambda i, *_: (row_fn(i) * 6 + which, 0, 0))


def _path(x, mod, row_of_seq, shared_mod, rope_tab, wts, alpha, cache=None):
    B, T, D = x.shape
    M = B * T
    x2 = x.reshape(M, D)
    heads = wts["heads"]
    qr, kvr = wts["qr"], wts["kvr"]
    CW = wts["cw"]
    MW = heads * V_DIM
    tm = min(TOKEN_TILE, M) if shared_mod else min(TOKEN_TILE, T)
    assert M % tm == 0 and (T % tm == 0 or tm % T == 0)
    assert rope_tab.shape[0] == max(T, tm)

    def row_fn(i):
        return row_of_seq((i * tm) // T)

    def rope_idx(i, *_):
        return (i % max(T // tm, 1), 0)

    mspec = functools.partial(_mod_spec, D, row_fn=row_fn)
    row_spec = pl.BlockSpec((tm, D), lambda i, *_: (i, 0))

    wa = wts["w_a"]
    NA = wa.shape[1]
    qcn, ckv, kpe, kpad = pl.pallas_call(
        functools.partial(_inproj_a_kernel, qr=qr, kvr=kvr),
        grid=(M // tm,),
        in_specs=[row_spec, mspec(1), mspec(0),
                  pl.BlockSpec((D, NA), lambda i: (0, 0)),
                  pl.BlockSpec((1, qr), lambda i: (0, 0)),
                  pl.BlockSpec((1, kvr), lambda i: (0, 0)),
                  pl.BlockSpec((tm, 2 * LANES), rope_idx)],
        out_specs=[pl.BlockSpec((tm, qr), lambda i: (i, 0)),
                   pl.BlockSpec((tm, kvr), lambda i: (i, 0)),
                   pl.BlockSpec((tm, ROPE_DIM), lambda i: (i, 0)),
                   pl.BlockSpec((tm, LANES), lambda i: (i, 0))],
        out_shape=[jax.ShapeDtypeStruct((M, qr), BF),
                   jax.ShapeDtypeStruct((M, kvr), F32),
                   jax.ShapeDtypeStruct((M, ROPE_DIM), F32),
                   jax.ShapeDtypeStruct((M, LANES), BF)],
        compiler_params=_cp(("arbitrary",), 56),
        name="inproj_a",
    )(x2, mod, mod, wa, wts["g_q"], wts["g_kv"], rope_tab)

    wu = wts["w_u"]
    tn = min(512, CW)
    y = pl.pallas_call(
        _inproj_u_kernel,
        grid=(M // tm, CW // tn),
        in_specs=[row_spec, mspec(1), mspec(0),
                  pl.BlockSpec((D, tn), lambda i, j: (0, j)),
                  pl.BlockSpec((D, tn), lambda i, j: (0, j + CW // tn))],
        out_specs=pl.BlockSpec((tm, tn), lambda i, j: (i, j)),
        out_shape=jax.ShapeDtypeStruct((M, CW), F32),
        scratch_shapes=[pltpu.VMEM((tm, D), BF)],
        compiler_params=_cp(("arbitrary", "arbitrary"), 48),
        name="inproj_u",
    )(x2, mod, mod, wu, wu)

    wq = wts["w_q"]
    q = pl.pallas_call(
        functools.partial(_qproj_kernel, heads=heads),
        grid=(M // tm,),
        in_specs=[pl.BlockSpec((tm, qr), lambda i: (i, 0)),
                  pl.BlockSpec(wq.shape, lambda i: (0, 0)),
                  pl.BlockSpec((tm, 2 * LANES), rope_idx)],
        out_specs=pl.BlockSpec((tm, heads * HEAD_Q), lambda i: (i, 0)),
        out_shape=jax.ShapeDtypeStruct((M, heads * HEAD_Q), BF),
        compiler_params=_cp(("arbitrary",), 48),
        name="qproj",
    )(qcn, wq, rope_tab)

    if cache is not None:
        cache_ckv, cache_kpe = cache
        P = cache_ckv.shape[1]
        ckv_all = jnp.concatenate([cache_ckv, ckv.reshape(B, T, kvr)], axis=1)
        kpad_all = jnp.concatenate(
            [jnp.pad(cache_kpe, ((0, 0), (0, 0), (0, LANES - ROPE_DIM))).astype(BF),
             kpad.reshape(B, T, LANES)], axis=1)
    else:
        P = 0
        ckv_all = ckv.reshape(B, T, kvr)
        kpad_all = kpad.reshape(B, T, LANES)
    S = P + T
    MS = B * S
    tk = math.gcd(MS, TOKEN_TILE)
    wkv = wts["w_kv"]
    kv = pl.pallas_call(
        functools.partial(_kvproj_kernel, heads=heads),
        grid=(MS // tk,),
        in_specs=[pl.BlockSpec((tk, kvr), lambda i: (i, 0)),
                  pl.BlockSpec(wkv.shape, lambda i: (0, 0))],
        out_specs=pl.BlockSpec((tk, heads * HEAD_KV), lambda i: (i, 0)),
        out_shape=jax.ShapeDtypeStruct((MS, heads * HEAD_KV), BF),
        compiler_params=_cp(("arbitrary",), 40),
        name="kvproj",
    )(ckv_all.reshape(MS, kvr), wkv)

    tq = min(ATTN_Q_TILE, T)
    attn = pl.pallas_call(
        functools.partial(_attn_kernel, heads=heads, scale=(NOPE_DIM + ROPE_DIM) ** -0.5),
        grid=(B, T // tq),
        in_specs=[pl.BlockSpec((1, tq, heads * HEAD_Q), lambda b, t: (b, t, 0)),
                  pl.BlockSpec((1, S, heads * HEAD_KV), lambda b, t: (b, 0, 0)),
                  pl.BlockSpec((1, S, LANES), lambda b, t: (b, 0, 0))],
        out_specs=pl.BlockSpec((1, tq, MW), lambda b, t: (b, t, 0)),
        out_shape=jax.ShapeDtypeStruct((B, T, MW), BF),
        compiler_params=_cp(("arbitrary", "arbitrary"), 48),
        name="attn",
    )(q.reshape(B, T, heads * HEAD_Q), kv.reshape(B, S, heads * HEAD_KV), kpad_all)

    tt = min(CONV_T_TILE, T)
    taps = wts["w_dw"].shape[0]
    conv = pl.pallas_call(
        functools.partial(_conv_kernel, tt=tt, taps=taps),
        grid=(B, T // tt),
        in_specs=[pl.BlockSpec((1, T, CW), lambda b, t: (b, 0, 0)),
                  pl.BlockSpec((taps, CW), lambda b, t: (0, 0)),
                  pl.BlockSpec((1, CW), lambda b, t: (0, 0)),
                  pl.BlockSpec((1, CW), lambda b, t: (0, 0)),
                  pl.BlockSpec((1, CW), lambda b, t: (0, 0))],
        out_specs=pl.BlockSpec((1, tt, CW), lambda b, t: (b, t, 0)),
        out_shape=jax.ShapeDtypeStruct((B, T, CW), BF),
        scratch_shapes=[pltpu.VMEM((tt + 2 * CONV_HALO, CW), F32), pltpu.VMEM((tt, CW), F32)],
        compiler_params=_cp(("arbitrary", "arbitrary"), 40),
        name="conv",
    )(y.reshape(B, T, CW), wts["w_dw"], wts["b_dw"], wts["g_cn"], wts["b_cn"])

    wo = wts["w_out"]
    tkk = min(512, MW)
    na = MW // tkk
    nk = na + CW // tkk
    vec_spec = pl.BlockSpec((1, D), lambda i, *_: (0, 0))
    x1 = pl.pallas_call(
        functools.partial(_outproj_kernel, na=na, nk=nk, alpha=alpha),
        grid=(M // tm, nk),
        in_specs=[pl.BlockSpec((tm, tkk), lambda i, k: (i, jnp.minimum(k, na - 1))),
                  pl.BlockSpec((tm, tkk), lambda i, k: (i, jnp.maximum(k - na, 0))),
                  pl.BlockSpec((tkk, D), lambda i, k: (k, 0)),
                  row_spec, mspec(2), vec_spec, vec_spec],
        out_specs=row_spec,
        out_shape=jax.ShapeDtypeStruct((M, D), F32),
        compiler_params=_cp(("arbitrary", "arbitrary"), 56),
        name="outproj",
    )(attn.reshape(M, MW), conv.reshape(M, CW), wo, x2, mod, wts["ln1_g"], wts["ln1_b"])

    wpq = wts["w_pq"]
    keys_bf = wts["sub_keys"]
    nhp, nkeys, half = keys_bf.shape
    ph = nhp // 2
    gq = math.gcd(nhp, PEER_Q_GROUP)
    h2, sT = pl.pallas_call(
        _peer_q_kernel,
        grid=(M // tm, nhp // gq),
        in_specs=[row_spec, mspec(4), mspec(3),
                  pl.BlockSpec((D, gq * half), lambda i, j: (0, j)),
                  pl.BlockSpec((gq, nkeys, half), lambda i, j: (j, 0, 0))],
        out_specs=[pl.BlockSpec((tm, D), lambda i, j: (i, 0)),
                   pl.BlockSpec((gq, nkeys, tm), lambda i, j: (j, 0, i))],
        out_shape=[jax.ShapeDtypeStruct((M, D), BF),
                   jax.ShapeDtypeStruct((nhp, nkeys, M), F32)],
        compiler_params=_cp(("arbitrary", "arbitrary"), 48),
        name="peer_q",
    )(x1, mod, mod, wpq, keys_bf)

    tl = min(256, M)
    stats, e2 = pl.pallas_call(
        _route_kernel,
        grid=(M // tl, ph),
        in_specs=[pl.BlockSpec((2, nkeys, tl), lambda i, h: (h, 0, i))],
        out_specs=[pl.BlockSpec((1, SUBLANES, tl), lambda i, h: (h, 0, i)),
                   pl.BlockSpec((1, nkeys, tl), lambda i, h: (h, 0, i))],
        out_shape=[jax.ShapeDtypeStruct((ph, SUBLANES, M), F32),
                   jax.ShapeDtypeStruct((ph, nkeys, M), F32)],
        scratch_shapes=[pltpu.VMEM((24, tl), F32),
                        pltpu.VMEM((-(-_n_cand(PEER_TOPK + 1) // SUBLANES) * SUBLANES, tl), F32)],
        compiler_params=_cp(("arbitrary", "arbitrary"), 32),
        name="peer_route",
    )(sT)

    pu, pv = wts["peer_u"], wts["peer_v"]
    NE = pu.shape[0]
    nt = min(PEER_TOKEN_TILE, M)
    et = min(PEER_EXPERT_TILE, NE)
    rows = et // nkeys
    assert et % nkeys == 0 and NE == nkeys * nkeys and rows % SUBLANES == 0
    ff = pl.pallas_call(
        functools.partial(_peer_kernel, heads=ph, keys=nkeys),
        grid=(M // nt, NE // et),
        in_specs=[pl.BlockSpec((nt, D), lambda i, e: (i, 0)),
                  pl.BlockSpec((ph, None, rows, nt), lambda i, e: (0, 0, e, i)),
                  pl.BlockSpec((ph, nkeys, nt), lambda i, e: (0, 0, i)),
                  pl.BlockSpec((ph, SUBLANES, nt), lambda i, e: (0, 0, i)),
                  pl.BlockSpec((et, D), lambda i, e: (e, 0)),
                  pl.BlockSpec((et, D), lambda i, e: (e, 0))],
        out_specs=pl.BlockSpec((nt, D), lambda i, e: (i, 0), pipeline_mode=pl.Buffered(1)),
        out_shape=jax.ShapeDtypeStruct((M, D), F32),
        scratch_shapes=[pltpu.VMEM((et, nt), F32),
                        pltpu.VMEM((nt, et), BF)],
        compiler_params=_cp(("arbitrary", "arbitrary"), 60),
        name="peer_main",
    )(h2, sT.reshape(ph, 2, nkeys, M), e2, stats, pu, pv)

    t2 = min(256, T)

    def row_fn2(i):
        return row_of_seq((i * t2) // T)

    row2 = pl.BlockSpec((t2, D), lambda i: (i, 0))
    y_out = pl.pallas_call(
        functools.partial(_ln2_kernel, alpha=alpha),
        grid=(M // t2,),
        in_specs=[row2, row2, _mod_spec(D, 5, row_fn2),
                  pl.BlockSpec((1, D), lambda i: (0, 0)), pl.BlockSpec((1, D), lambda i: (0, 0))],
        out_specs=row2,
        out_shape=jax.ShapeDtypeStruct((M, D), F32),
        compiler_params=_cp(("arbitrary",), 40),
        name="ln2",
    )(x1, ff, mod, wts["ln2_g"], wts["ln2_b"])

    return y_out.reshape(B, T, D), ckv.reshape(B, T, kvr), kpe.reshape(B, T, ROPE_DIM)


def _rope_table(n_tokens):
    rows = n_tokens // GRID_W
    row = jnp.repeat(jnp.arange(rows, dtype=F32), GRID_W)
    col = jnp.tile(jnp.arange(GRID_W, dtype=F32), rows)
    n_freq = ROPE_DIM // 4
    inv = ROPE_BASE ** (-jnp.arange(n_freq, dtype=F32) / n_freq)
    ang_r = row[:, None] * inv
    ang_c = col[:, None] * inv
    ang = jnp.concatenate([ang_r, ang_r, ang_c, ang_c], -1)
    return _lane_pad_table(jnp.cos(ang), jnp.sin(ang))


def _lane_pad_table(cos, sin):
    zero = jnp.zeros((cos.shape[0], LANES - ROPE_DIM), F32)
    return jnp.concatenate([cos, zero, sin, zero], -1)


def _prep_weights(l, w_in, g_q, w_uq, g_kv, w_ukv, w_dw, b_dw, g_cn, b_cn, w_out,
                  ln1_g, ln1_b, w_pq, sub_keys, peer_u, peer_v, ln2_g, ln2_b):
    qr = g_q.shape[-1]
    kvr = g_kv.shape[-1]
    o3 = qr + kvr + ROPE_DIM
    heads = w_uq.shape[-1] // (NOPE_DIM + ROPE_DIM)
    wi = w_in[l]
    n_a = qr + kvr + LANES
    assert (qr + kvr) % LANES == 0 and n_a <= wi.shape[1]
    wi_bf = wi.astype(BF)
    w_a = wi_bf[:, :n_a]
    w_u = wi_bf[:, o3:]
    wq3 = w_uq[l].reshape(qr, heads, NOPE_DIM + ROPE_DIM).astype(BF)
    w_q = jnp.pad(wq3, ((0, 0), (0, 0), (0, HEAD_Q - NOPE_DIM - ROPE_DIM)))
    w_q = w_q.reshape(qr, heads * HEAD_Q)
    ph, two, nkeys, half = sub_keys.shape[1:]
    return dict(
        heads=heads, qr=qr, kvr=kvr, cw=w_dw.shape[-1],
        w_a=w_a, w_u=w_u, w_q=w_q, w_kv=w_ukv[l].astype(BF),
        g_q=g_q[l][None], g_kv=g_kv[l][None],
        w_dw=w_dw[l], b_dw=b_dw[l][None], g_cn=g_cn[l][None], b_cn=b_cn[l][None],
        w_out=w_out[l].astype(BF), ln1_g=ln1_g[l][None], ln1_b=ln1_b[l][None],
        w_pq=w_pq[l].astype(BF), sub_keys=sub_keys[l].reshape(ph * two, nkeys, half).astype(BF),
        peer_u=peer_u[l].astype(BF), peer_v=peer_v[l].astype(BF),
        ln2_g=ln2_g[l][None], ln2_b=ln2_b[l][None],
    )


def kernel(x_prompt, x_sample, cache_ckv, cache_kpe, c, c_ctx, w_ada, b_ada, w_in, g_q, w_uq,
           g_kv, w_ukv, w_dw, b_dw, g_cn, b_cn, w_out, ln1_g, ln1_b, w_pq, sub_keys, peer_u,
           peer_v, ln2_g, ln2_b):
    depth = w_ada.shape[0]
    alpha = (2 * depth) ** 0.25
    B, T, D = x_prompt.shape
    Bd, Td, _ = x_sample.shape
    n_rows = -(-(1 + Bd) // SUBLANES) * SUBLANES
    cond = jnp.concatenate([c_ctx[None, :], c, jnp.zeros((n_rows - 1 - Bd, D), F32)], axis=0)

    n_tab = max(T, min(TOKEN_TILE, B * T))
    ones_tab = _lane_pad_table(jnp.ones((n_tab, ROPE_DIM), F32), jnp.zeros((n_tab, ROPE_DIM), F32))
    rope_tab = _rope_table(Td)

    xp, xs = x_prompt, x_sample
    ckv_layers, kpe_layers = [], []
    for l in range(depth):
        wts = _prep_weights(l, w_in, g_q, w_uq, g_kv, w_ukv, w_dw, b_dw, g_cn, b_cn, w_out,
                            ln1_g, ln1_b, w_pq, sub_keys, peer_u, peer_v, ln2_g, ln2_b)
        mod = _adaln(cond, w_ada[l], b_ada[l]).reshape(n_rows * 6, 1, D)
        xp, ckv, kpe = _path(xp, mod, lambda b: 0, True, ones_tab, wts, alpha)
        ckv_layers.append(ckv)
        kpe_layers.append(kpe)
        xs, _, _ = _path(xs, mod, lambda b: 1 + b, False, rope_tab, wts, alpha,
                         cache=(cache_ckv[:, l], cache_kpe[:, l]))
    return (xp, xs, jnp.stack(ckv_layers, axis=1), jnp.stack(kpe_layers, axis=1))
```

```python
import functools
import math

import jax
import jax.numpy as jnp
from jax import lax
from jax.experimental import pallas as pl
from jax.experimental.pallas import tpu as pltpu

F32 = jnp.float32
BF = jnp.bfloat16

NOPE_DIM = 128
ROPE_DIM = 64
V_DIM = 128
HEAD_Q = NOPE_DIM + 2 * ROPE_DIM
HEAD_KV = NOPE_DIM + V_DIM
GRID_W = 64
ROPE_BASE = 10000.0
PEER_TOPK = 16
EPS = 1e-6
LOG2_E = 1.4426950408889634

LANES = 128
SUBLANES = 8
VMEM_MIB = 1 << 20

TOKEN_TILE = 512
OUTPROJ_K_TILE = 512
PEER_TOKEN_TILE = 512
PEER_EXPERT_TILE = 1024
PEER_Q_GROUP = 4
DOT_HEAD_GROUP = 4
ATTN_Q_TILE = 256
CONV_T_TILE = 256
CONV_T_CHUNK = 32
CONV_C_CHUNK = 256
CONV_HALO = 16


def _cp(sem, vmem_mib):
    return pltpu.CompilerParams(dimension_semantics=sem, vmem_limit_bytes=vmem_mib * VMEM_MIB)


def _rope_lanes(x, tab):
    q = ROPE_DIM // 4
    lane = lax.broadcasted_iota(jnp.int32, x.shape, 1)
    x = jnp.where(lane < ROPE_DIM, x, 0.0)
    nxt = pltpu.roll(x, LANES - q, 1)
    prv = pltpu.roll(x, q, 1)
    rot = jnp.where((lane // q) % 2 == 0, -nxt, prv)
    return x * tab[:, :LANES] + rot * tab[:, LANES:]


def _layer_norm_rows(z, g, b):
    mu = jnp.mean(z, -1, keepdims=True)
    zc = z - mu
    var = jnp.mean(zc * zc, -1, keepdims=True)
    return zc * lax.rsqrt(var + EPS) * g + b


def _rms_norm_rows(z, g):
    return z * lax.rsqrt(jnp.mean(z * z, -1, keepdims=True) + EPS) * g


def _adaln_kernel(c_ref, w_ref, b_ref, o_ref):
    c = c_ref[...]
    s = (c * jax.nn.sigmoid(c)).astype(BF)
    o_ref[...] = jnp.dot(s, w_ref[...].astype(BF), preferred_element_type=F32) + b_ref[...]


def _adaln(cond, w, b):
    R, D = cond.shape
    N = w.shape[1]
    tn = min(512, N)
    return pl.pallas_call(
        _adaln_kernel,
        grid=(N // tn,),
        in_specs=[pl.BlockSpec((R, D), lambda j: (0, 0)),
                  pl.BlockSpec((D, tn), lambda j: (0, j)),
                  pl.BlockSpec((1, tn), lambda j: (0, j))],
        out_specs=pl.BlockSpec((R, tn), lambda j: (0, j)),
        out_shape=jax.ShapeDtypeStruct((R, N), F32),
        compiler_params=_cp(("arbitrary",), 40),
        name="adaln",
    )(cond, w, b.reshape(1, N))


def _inproj_a_kernel(x_ref, sc_ref, sh_ref, w_ref, gq_ref, gkv_ref, rope_ref,
                     qcn_ref, ckv_ref, kpe_ref, kpad_ref, h_ref, *, qr, kvr):
    h_ref[...] = (x_ref[...] * (1.0 + sc_ref[0]) + sh_ref[0]).astype(BF)
    p = jnp.dot(h_ref[...], w_ref[...], preferred_element_type=F32)
    qcn_ref[...] = _rms_norm_rows(p[:, :qr], gq_ref[...]).astype(BF)
    ckv_ref[...] = _rms_norm_rows(p[:, qr:qr + kvr], gkv_ref[...])
    kslab = p[:, qr + kvr:qr + kvr + LANES]
    kpe_ref[...] = kslab[:, :ROPE_DIM]
    kpad_ref[...] = _rope_lanes(kslab, rope_ref[...]).astype(BF)


def _inproj_u_kernel(h_ref, wa_ref, wg_ref, y_ref):
    h = h_ref[...]
    a = jnp.dot(h, wa_ref[...], preferred_element_type=F32)
    g = jnp.dot(h, wg_ref[...], preferred_element_type=F32)
    y_ref[...] = a * jax.nn.sigmoid(g)


def _qproj_kernel(x_ref, w_ref, rope_ref, o_ref, *, heads):
    x = x_ref[...]
    rope = rope_ref[...]
    group = math.gcd(heads, DOT_HEAD_GROUP)
    for g in range(heads // group):
        qg = jnp.dot(x, w_ref[:, g * group * HEAD_Q:(g + 1) * group * HEAD_Q],
                     preferred_element_type=F32)
        for j in range(group):
            b = (g * group + j) * HEAD_Q
            q = qg[:, j * HEAD_Q:(j + 1) * HEAD_Q]
            o_ref[:, b:b + NOPE_DIM] = q[:, :NOPE_DIM].astype(BF)
            o_ref[:, b + NOPE_DIM:b + HEAD_Q] = _rope_lanes(q[:, NOPE_DIM:], rope).astype(BF)


def _kvproj_kernel(x_ref, w_ref, o_ref, *, heads):
    x = x_ref[...].astype(BF)
    width = math.gcd(heads, DOT_HEAD_GROUP) * HEAD_KV
    for g in range(heads * HEAD_KV // width):
        o_ref[:, g * width:(g + 1) * width] = jnp.dot(
            x, w_ref[:, g * width:(g + 1) * width], preferred_element_type=F32).astype(BF)


def _attn_kernel(q_ref, kv_ref, kpad_ref, o_ref, *, heads, scale):
    kpad = kpad_ref[0]
    for h in range(heads):
        qh = q_ref[0, :, h * HEAD_Q:(h + 1) * HEAD_Q]
        kh = jnp.concatenate([kv_ref[0, :, h * HEAD_KV:h * HEAD_KV + NOPE_DIM], kpad], axis=1)
        s = lax.dot_general(qh, kh, (((1,), (1,)), ((), ())), preferred_element_type=F32)
        m = jnp.max(s, -1, keepdims=True)
        p = jnp.exp2((s - m) * (scale * LOG2_E))
        l = jnp.sum(p, -1, keepdims=True)
        o = jnp.dot(p.astype(BF), kv_ref[0, :, h * HEAD_KV + NOPE_DIM:(h + 1) * HEAD_KV],
                    preferred_element_type=F32)
        o_ref[0, :, h * V_DIM:(h + 1) * V_DIM] = (o / l).astype(BF)


def _conv_kernel(y_ref, w_ref, bdw_ref, g_ref, b_ref, o_ref, pad_scr, z_scr, *, tt, taps):
    t = pl.program_id(1)
    nt = pl.num_programs(1)
    C = z_scr.shape[1]
    half = taps // 2
    pad_scr[CONV_HALO:CONV_HALO + tt, :] = y_ref[0, pl.ds(pl.multiple_of(t * tt, tt), tt), :]

    @pl.when(t == 0)
    def _():
        pad_scr[0:CONV_HALO, :] = jnp.zeros((CONV_HALO, C), F32)

    @pl.when(t > 0)
    def _():
        pad_scr[0:CONV_HALO, :] = y_ref[
            0, pl.ds(pl.multiple_of(t * tt - CONV_HALO, SUBLANES), CONV_HALO), :]

    @pl.when(t == nt - 1)
    def _():
        pad_scr[CONV_HALO + tt:, :] = jnp.zeros((CONV_HALO, C), F32)

    @pl.when(t < nt - 1)
    def _():
        pad_scr[CONV_HALO + tt:, :] = y_ref[
            0, pl.ds(pl.multiple_of((t + 1) * tt, tt), CONV_HALO), :]

    tc = CONV_T_CHUNK
    cc = min(CONV_C_CHUNK, C)
    base = CONV_HALO - half
    wlen = tc + 2 * CONV_HALO
    assert base >= 0 and base + taps - 1 + tc <= wlen

    def chan_body(ci, carry):
        c0 = pl.multiple_of(ci * cc, cc)
        for r in range(tt // tc):
            win = pad_scr[r * tc:r * tc + wlen, pl.ds(c0, cc)]
            acc = jnp.zeros((tc, cc), F32)
            for sh in range(SUBLANES):
                rolled = win if sh == 0 else pltpu.roll(win, wlen - sh, 0)
                for a in range(wlen // SUBLANES):
                    k = a * SUBLANES + sh - base
                    if 0 <= k < taps:
                        acc = acc + (rolled[a * SUBLANES:a * SUBLANES + tc, :]
                                     * w_ref[k:k + 1, pl.ds(c0, cc)])
            z_scr[r * tc:(r + 1) * tc, pl.ds(c0, cc)] = acc + bdw_ref[:, pl.ds(c0, cc)]
        return carry

    lax.fori_loop(0, C // cc, chan_body, 0)

    rows = 64
    for r in range(tt // rows):
        zn = _layer_norm_rows(z_scr[r * rows:(r + 1) * rows, :], g_ref[...], b_ref[...])
        o_ref[0, r * rows:(r + 1) * rows, :] = (zn * jax.nn.sigmoid(zn)).astype(BF)


def _ln_inplace(xo_ref, x_ref, gate_ref, lg_ref, lb_ref, ff_ref, alpha):
    rows = 64
    for r in range(xo_ref.shape[0] // rows):
        sl = slice(r * rows, (r + 1) * rows)
        z = alpha * x_ref[sl, :] + gate_ref[0] * ff_ref[sl, :]
        xo_ref[sl, :] = _layer_norm_rows(z, lg_ref[...], lb_ref[...])


def _outproj_kernel(a_ref, c_ref, w_ref, x_ref, g1_ref, lg_ref, lb_ref, xo_ref, *, na, nk, alpha):
    k = pl.program_id(1)

    @pl.when(k == 0)
    def _():
        xo_ref[...] = jnp.zeros(xo_ref.shape, F32)

    def accumulate(lhs_ref):
        lhs = lhs_ref[...]
        nc = min(512, xo_ref.shape[1])
        for n in range(xo_ref.shape[1] // nc):
            cs = slice(n * nc, (n + 1) * nc)
            xo_ref[:, cs] += jnp.dot(lhs, w_ref[:, cs], preferred_element_type=F32)

    @pl.when(k < na)
    def _():
        accumulate(a_ref)

    @pl.when(k >= na)
    def _():
        accumulate(c_ref)

    @pl.when(k == nk - 1)
    def _():
        _ln_inplace(xo_ref, x_ref, g1_ref, lg_ref, lb_ref, xo_ref, alpha)


def _peer_q_kernel(x_ref, sc_ref, sh_ref, w_ref, k_ref, h_ref, s_ref):
    @pl.when(pl.program_id(1) == 0)
    def _():
        h_ref[...] = (x_ref[...] * (1.0 + sc_ref[0]) + sh_ref[0]).astype(BF)

    half = k_ref.shape[2]
    q = jnp.dot(h_ref[...], w_ref[...], preferred_element_type=F32).astype(BF)
    for p in range(k_ref.shape[0]):
        s_ref[p] = lax.dot_general(k_ref[p], q[:, p * half:(p + 1) * half],
                                   (((1,), (1,)), ((), ())), preferred_element_type=F32)


def _route_kernel(s_ref, st_ref, e2_ref, t2_scr, cand_scr):
    n_top = PEER_TOPK + 1
    neg = -jnp.inf
    tl = s_ref.shape[2]

    def top(p):
        n_grp = s_ref.shape[1] // SUBLANES
        col = [s_ref[p, g * SUBLANES:(g + 1) * SUBLANES, :] for g in range(n_grp)]
        for i, j in _sorting_network(n_grp):
            col[i], col[j] = jnp.maximum(col[i], col[j]), jnp.minimum(col[i], col[j])
        col.append(jnp.full(col[0].shape, neg, F32))
        out = []
        for it in range(n_top):
            m = jnp.max(col[0], axis=0, keepdims=True)
            out.append(m)
            took = col[0] == m
            for g in range(min(n_grp, n_top - 1 - it)):
                col[g] = jnp.where(took, col[g + 1], col[g])
        return out

    t1 = top(0)
    t2 = top(1)
    for k in range(n_top):
        t2_scr[k:k + 1, :] = t2[k]
    cand_scr[...] = jnp.full(cand_scr.shape, neg, F32)
    off = 0
    for a in range(n_top):
        nb = n_top // (a + 1)
        cand_scr[off:off + nb, :] = t2_scr[0:nb, :] + t1[a]
        off += nb
    cand = cand_scr[...]
    c = cand
    kth = None
    nxt = None
    for it in range(n_top):
        m = jnp.max(c, axis=0, keepdims=True)
        if it == PEER_TOPK - 1:
            kth = m
        if it == PEER_TOPK:
            nxt = m
        c = jnp.where(c == m, neg, c)
    tau = 0.5 * (kth + nxt)
    top_sum = t1[0] + t2[0]
    z = jnp.sum(jnp.where(cand > tau, jnp.exp(cand - top_sum), 0.0), axis=0, keepdims=True)
    st_ref[0, 0:1, :] = tau
    st_ref[0, 1:2, :] = t1[0]
    log_norm = t2[0] + jnp.log(z)
    st_ref[0, 2:3, :] = log_norm
    st_ref[0, 3:, :] = jnp.zeros((SUBLANES - 3, tl), F32)
    e2_ref[0] = jnp.exp(s_ref[1] - log_norm)


def _sorting_network(n):
    assert n & (n - 1) == 0

    def merge(lo, hi, r):
        step = r * 2
        if step < hi - lo:
            yield from merge(lo, hi, step)
            yield from merge(lo + r, hi, step)
            yield from ((i, i + r) for i in range(lo + r, hi - r, step))
        else:
            yield (lo, lo + r)

    def sort(lo, hi):
        if hi - lo >= 1:
            mid = lo + (hi - lo) // 2
            yield from sort(lo, mid)
            yield from sort(mid + 1, hi)
            yield from merge(lo, hi, 1)

    return list(sort(0, n - 1))


def _n_cand(n_top):
    return sum(n_top // (a + 1) for a in range(n_top))


def _gelu_exact(x):
    return 0.5 * x * (1.0 + lax.erf(x * (1.0 / math.sqrt(2.0))))


def _peer_kernel(h_ref, s1_ref, e2_ref, st_ref, u_ref, v_ref, o_ref, a_scr, w_scr, *, heads, keys):
    nt = h_ref.shape[0]
    et = u_ref.shape[0]
    n_halves = 2 if nt >= 2 * LANES else 1
    half = nt // n_halves

    @pl.when(pl.program_id(1) == 0)
    def _():
        o_ref[...] = jnp.zeros(o_ref.shape, F32)

    def weight_block(ii, c):
        cols = slice(c * LANES, (c + 1) * LANES)
        w = jnp.zeros((keys, LANES), F32)
        for h in range(heads):
            s1 = s1_ref[h, ii:ii + 1, cols]
            e2_min = jnp.exp(st_ref[h, 0:1, cols] - s1 - st_ref[h, 2:3, cols])
            e1 = jnp.exp(s1 - st_ref[h, 1:2, cols])
            e2 = e2_ref[h, :, cols]
            w = w + jnp.where(e2 >= e2_min, e2, 0.0) * e1
        ars = slice(ii * keys, (ii + 1) * keys)
        w_scr[cols, ars] = (w * _gelu_exact(a_scr[ars, cols])).T.astype(BF)

    def up_rows(k):
        rs = slice(k * half, (k + 1) * half)
        w_rows = w_scr[rs, :]
        nc = min(512, o_ref.shape[1])
        for n in range(o_ref.shape[1] // nc):
            cs = slice(n * nc, (n + 1) * nc)
            o_ref[rs, cs] += jnp.dot(w_rows, v_ref[:, cs], preferred_element_type=F32)

    def down_cols(k):
        cols = slice(k * half, (k + 1) * half)
        a_scr[:, cols] = lax.dot_general(u_ref[...], h_ref[cols, :], (((1,), (1,)), ((), ())),
                                         preferred_element_type=F32)

    for k in range(n_halves):
        down_cols(k)
    for k in range(n_halves):
        for ii in range(et // keys):
            for c in range(k * half // LANES, (k + 1) * half // LANES):
                weight_block(ii, c)
        up_rows(k)


def _ln2_kernel(x_ref, ff_ref, g2_ref, lg_ref, lb_ref, o_ref, *, alpha):
    _ln_inplace(o_ref, x_ref, g2_ref, lg_ref, lb_ref, ff_ref, alpha)


def _mod_spec(D, which, row_fn):
    return pl.BlockSpec((1, 1, D), lambda i, *_: (row_fn(i) * 6 + which, 0, 0))


def _path(x, mod, row_of_seq, shared_mod, rope_tab, wts, alpha, cache=None):
    B, T, D = x.shape
    M = B * T
    x2 = x.reshape(M, D)
    heads = wts["heads"]
    qr, kvr = wts["qr"], wts["kvr"]
    CW = wts["cw"]
    MW = heads * V_DIM
    tm = min(TOKEN_TILE, M) if shared_mod else min(TOKEN_TILE, T)
    assert M % tm == 0 and (T % tm == 0 or tm % T == 0)
    assert rope_tab.shape[0] == max(T, tm)

    def row_fn(i):
        return row_of_seq((i * tm) // T)

    def rope_idx(i, *_):
        return (i % max(T // tm, 1), 0)

    mspec = functools.partial(_mod_spec, D, row_fn=row_fn)
    row_spec = pl.BlockSpec((tm, D), lambda i, *_: (i, 0))

    wa = wts["w_a"]
    NA = wa.shape[1]
    qcn, ckv, kpe, kpad, h1 = pl.pallas_call(
        functools.partial(_inproj_a_kernel, qr=qr, kvr=kvr),
        grid=(M // tm,),
        in_specs=[row_spec, mspec(1), mspec(0),
                  pl.BlockSpec((D, NA), lambda i: (0, 0), pipeline_mode=pl.Buffered(1)),
                  pl.BlockSpec((1, qr), lambda i: (0, 0)),
                  pl.BlockSpec((1, kvr), lambda i: (0, 0)),
                  pl.BlockSpec((tm, 2 * LANES), rope_idx)],
        out_specs=[pl.BlockSpec((tm, qr), lambda i: (i, 0)),
                   pl.BlockSpec((tm, kvr), lambda i: (i, 0)),
                   pl.BlockSpec((tm, ROPE_DIM), lambda i: (i, 0)),
                   pl.BlockSpec((tm, LANES), lambda i: (i, 0)),
                   row_spec],
        out_shape=[jax.ShapeDtypeStruct((M, qr), BF),
                   jax.ShapeDtypeStruct((M, kvr), F32),
                   jax.ShapeDtypeStruct((M, ROPE_DIM), F32),
                   jax.ShapeDtypeStruct((M, LANES), BF),
                   jax.ShapeDtypeStruct((M, D), BF)],
        compiler_params=_cp(("arbitrary",), 56),
        name="inproj_a",
    )(x2, mod, mod, wa, wts["g_q"], wts["g_kv"], rope_tab)

    wu = wts["w_u"]
    tn = min(512, CW)
    y = pl.pallas_call(
        _inproj_u_kernel,
        grid=(M // tm, CW // tn),
        in_specs=[row_spec,
                  pl.BlockSpec((D, tn), lambda i, j: (0, j)),
                  pl.BlockSpec((D, tn), lambda i, j: (0, j + CW // tn))],
        out_specs=pl.BlockSpec((tm, tn), lambda i, j: (i, j)),
        out_shape=jax.ShapeDtypeStruct((M, CW), F32),
        compiler_params=_cp(("arbitrary", "arbitrary"), 48),
        name="inproj_u",
    )(h1, wu, wu)

    wq = wts["w_q"]
    q = pl.pallas_call(
        functools.partial(_qproj_kernel, heads=heads),
        grid=(M // tm,),
        in_specs=[pl.BlockSpec((tm, qr), lambda i: (i, 0)),
                  pl.BlockSpec(wq.shape, lambda i: (0, 0)),
                  pl.BlockSpec((tm, 2 * LANES), rope_idx)],
        out_specs=pl.BlockSpec((tm, heads * HEAD_Q), lambda i: (i, 0)),
        out_shape=jax.ShapeDtypeStruct((M, heads * HEAD_Q), BF),
        compiler_params=_cp(("arbitrary",), 48),
        name="qproj",
    )(qcn, wq, rope_tab)

    if cache is not None:
        cache_ckv, cache_kpe = cache
        P = cache_ckv.shape[1]
        ckv_all = jnp.concatenate([cache_ckv, ckv.reshape(B, T, kvr)], axis=1)
        kpad_all = jnp.concatenate(
            [jnp.pad(cache_kpe, ((0, 0), (0, 0), (0, LANES - ROPE_DIM))).astype(BF),
             kpad.reshape(B, T, LANES)], axis=1)
    else:
        P = 0
        ckv_all = ckv.reshape(B, T, kvr)
        kpad_all = kpad.reshape(B, T, LANES)
    S = P + T
    MS = B * S
    tk = math.gcd(MS, TOKEN_TILE)
    wkv = wts["w_kv"]
    kv = pl.pallas_call(
        functools.partial(_kvproj_kernel, heads=heads),
        grid=(MS // tk,),
        in_specs=[pl.BlockSpec((tk, kvr), lambda i: (i, 0)),
                  pl.BlockSpec(wkv.shape, lambda i: (0, 0))],
        out_specs=pl.BlockSpec((tk, heads * HEAD_KV), lambda i: (i, 0)),
        out_shape=jax.ShapeDtypeStruct((MS, heads * HEAD_KV), BF),
        compiler_params=_cp(("arbitrary",), 40),
        name="kvproj",
    )(ckv_all.reshape(MS, kvr), wkv)

    tq = min(ATTN_Q_TILE, T)
    attn = pl.pallas_call(
        functools.partial(_attn_kernel, heads=heads, scale=(NOPE_DIM + ROPE_DIM) ** -0.5),
        grid=(B, T // tq),
        in_specs=[pl.BlockSpec((1, tq, heads * HEAD_Q), lambda b, t: (b, t, 0)),
                  pl.BlockSpec((1, S, heads * HEAD_KV), lambda b, t: (b, 0, 0)),
                  pl.BlockSpec((1, S, LANES), lambda b, t: (b, 0, 0))],
        out_specs=pl.BlockSpec((1, tq, MW), lambda b, t: (b, t, 0)),
        out_shape=jax.ShapeDtypeStruct((B, T, MW), BF),
        compiler_params=_cp(("arbitrary", "arbitrary"), 48),
        name="attn",
    )(q.reshape(B, T, heads * HEAD_Q), kv.reshape(B, S, heads * HEAD_KV), kpad_all)

    tt = min(CONV_T_TILE, T)
    taps = wts["w_dw"].shape[0]
    conv = pl.pallas_call(
        functools.partial(_conv_kernel, tt=tt, taps=taps),
        grid=(B, T // tt),
        in_specs=[pl.BlockSpec((1, T, CW), lambda b, t: (b, 0, 0)),
                  pl.BlockSpec((taps, CW), lambda b, t: (0, 0)),
                  pl.BlockSpec((1, CW), lambda b, t: (0, 0)),
                  pl.BlockSpec((1, CW), lambda b, t: (0, 0)),
                  pl.BlockSpec((1, CW), lambda b, t: (0, 0))],
        out_specs=pl.BlockSpec((1, tt, CW), lambda b, t: (b, t, 0)),
        out_shape=jax.ShapeDtypeStruct((B, T, CW), BF),
        scratch_shapes=[pltpu.VMEM((tt + 2 * CONV_HALO, CW), F32), pltpu.VMEM((tt, CW), F32)],
        compiler_params=_cp(("arbitrary", "arbitrary"), 40),
        name="conv",
    )(y.reshape(B, T, CW), wts["w_dw"], wts["b_dw"], wts["g_cn"], wts["b_cn"])

    wo = wts["w_out"]
    tkk = min(OUTPROJ_K_TILE, MW)
    na = MW // tkk
    nk = na + CW // tkk
    vec_spec = pl.BlockSpec((1, D), lambda i, *_: (0, 0))
    x1 = pl.pallas_call(
        functools.partial(_outproj_kernel, na=na, nk=nk, alpha=alpha),
        grid=(M // tm, nk),
        in_specs=[pl.BlockSpec((tm, tkk), lambda i, k: (i, jnp.minimum(k, na - 1))),
                  pl.BlockSpec((tm, tkk), lambda i, k: (i, jnp.maximum(k - na, 0))),
                  pl.BlockSpec((tkk, D), lambda i, k: (k, 0)),
                  row_spec, mspec(2), vec_spec, vec_spec],
        out_specs=row_spec,
        out_shape=jax.ShapeDtypeStruct((M, D), F32),
        compiler_params=_cp(("arbitrary", "arbitrary"), 56),
        name="outproj",
    )(attn.reshape(M, MW), conv.reshape(M, CW), wo, x2, mod, wts["ln1_g"], wts["ln1_b"])

    wpq = wts["w_pq"]
    keys_bf = wts["sub_keys"]
    nhp, nkeys, half = keys_bf.shape
    ph = nhp // 2
    gq = math.gcd(nhp, PEER_Q_GROUP)
    h2, sT = pl.pallas_call(
        _peer_q_kernel,
        grid=(M // tm, nhp // gq),
        in_specs=[row_spec, mspec(4), mspec(3),
                  pl.BlockSpec((D, gq * half), lambda i, j: (0, j)),
                  pl.BlockSpec((gq, nkeys, half), lambda i, j: (j, 0, 0))],
        out_specs=[pl.BlockSpec((tm, D), lambda i, j: (i, 0)),
                   pl.BlockSpec((gq, nkeys, tm), lambda i, j: (j, 0, i))],
        out_shape=[jax.ShapeDtypeStruct((M, D), BF),
                   jax.ShapeDtypeStruct((nhp, nkeys, M), F32)],
        compiler_params=_cp(("arbitrary", "arbitrary"), 48),
        name="peer_q",
    )(x1, mod, mod, wpq, keys_bf)

    tl = min(256, M)
    stats, e2 = pl.pallas_call(
        _route_kernel,
        grid=(M // tl, ph),
        in_specs=[pl.BlockSpec((2, nkeys, tl), lambda i, h: (h, 0, i))],
        out_specs=[pl.BlockSpec((1, SUBLANES, tl), lambda i, h: (h, 0, i)),
                   pl.BlockSpec((1, nkeys, tl), lambda i, h: (h, 0, i))],
        out_shape=[jax.ShapeDtypeStruct((ph, SUBLANES, M), F32),
                   jax.ShapeDtypeStruct((ph, nkeys, M), F32)],
        scratch_shapes=[pltpu.VMEM((24, tl), F32),
                        pltpu.VMEM((-(-_n_cand(PEER_TOPK + 1) // SUBLANES) * SUBLANES, tl), F32)],
        compiler_params=_cp(("arbitrary", "arbitrary"), 32),
        name="peer_route",
    )(sT)

    pu, pv = wts["peer_u"], wts["peer_v"]
    NE = pu.shape[0]
    nt = min(PEER_TOKEN_TILE, M)
    et = min(PEER_EXPERT_TILE, NE)
    rows = et // nkeys
    assert et % nkeys == 0 and NE == nkeys * nkeys and rows % SUBLANES == 0
    ff = pl.pallas_call(
        functools.partial(_peer_kernel, heads=ph, keys=nkeys),
        grid=(M // nt, NE // et),
        in_specs=[pl.BlockSpec((nt, D), lambda i, e: (i, 0)),
                  pl.BlockSpec((ph, None, rows, nt), lambda i, e: (0, 0, e, i)),
                  pl.BlockSpec((ph, nkeys, nt), lambda i, e: (0, 0, i)),
                  pl.BlockSpec((ph, SUBLANES, nt), lambda i, e: (0, 0, i)),
                  pl.BlockSpec((et, D), lambda i, e: (e, 0)),
                  pl.BlockSpec((et, D), lambda i, e: (e, 0))],
        out_specs=pl.BlockSpec((nt, D), lambda i, e: (i, 0), pipeline_mode=pl.Buffered(1)),
        out_shape=jax.ShapeDtypeStruct((M, D), F32),
        scratch_shapes=[pltpu.VMEM((et, nt), F32),
                        pltpu.VMEM((nt, et), BF)],
        compiler_params=_cp(("arbitrary", "arbitrary"), 60),
        name="peer_main",
    )(h2, sT.reshape(ph, 2, nkeys, M), e2, stats, pu, pv)

    t2 = min(256, T)

    def row_fn2(i):
        return row_of_seq((i * t2) // T)

    row2 = pl.BlockSpec((t2, D), lambda i: (i, 0))
    y_out = pl.pallas_call(
        functools.partial(_ln2_kernel, alpha=alpha),
        grid=(M // t2,),
        in_specs=[row2, row2, _mod_spec(D, 5, row_fn2),
                  pl.BlockSpec((1, D), lambda i: (0, 0)), pl.BlockSpec((1, D), lambda i: (0, 0))],
        out_specs=row2,
        out_shape=jax.ShapeDtypeStruct((M, D), F32),
        compiler_params=_cp(("arbitrary",), 40),
        name="ln2",
    )(x1, ff, mod, wts["ln2_g"], wts["ln2_b"])

    return y_out.reshape(B, T, D), ckv.reshape(B, T, kvr), kpe.reshape(B, T, ROPE_DIM)


def _rope_table(n_tokens):
    rows = n_tokens // GRID_W
    row = jnp.repeat(jnp.arange(rows, dtype=F32), GRID_W)
    col = jnp.tile(jnp.arange(GRID_W, dtype=F32), rows)
    n_freq = ROPE_DIM // 4
    inv = ROPE_BASE ** (-jnp.arange(n_freq, dtype=F32) / n_freq)
    ang_r = row[:, None] * inv
    ang_c = col[:, None] * inv
    ang = jnp.concatenate([ang_r, ang_r, ang_c, ang_c], -1)
    return _lane_pad_table(jnp.cos(ang), jnp.sin(ang))


def _lane_pad_table(cos, sin):
    zero = jnp.zeros((cos.shape[0], LANES - ROPE_DIM), F32)
    return jnp.concatenate([cos, zero, sin, zero], -1)


def _prep_weights(l, w_in, g_q, w_uq, g_kv, w_ukv, w_dw, b_dw, g_cn, b_cn, w_out,
                  ln1_g, ln1_b, w_pq, sub_keys, peer_u, peer_v, ln2_g, ln2_b):
    qr = g_q.shape[-1]
    kvr = g_kv.shape[-1]
    o3 = qr + kvr + ROPE_DIM
    heads = w_uq.shape[-1] // (NOPE_DIM + ROPE_DIM)
    wi = w_in[l]
    n_a = qr + kvr + LANES
    assert (qr + kvr) % LANES == 0 and n_a <= wi.shape[1]
    wi_bf = wi.astype(BF)
    w_a = wi_bf[:, :n_a]
    w_u = wi_bf[:, o3:]
    wq3 = w_uq[l].reshape(qr, heads, NOPE_DIM + ROPE_DIM).astype(BF)
    w_q = jnp.pad(wq3, ((0, 0), (0, 0), (0, HEAD_Q - NOPE_DIM - ROPE_DIM)))
    w_q = w_q.reshape(qr, heads * HEAD_Q)
    ph, two, nkeys, half = sub_keys.shape[1:]
    return dict(
        heads=heads, qr=qr, kvr=kvr, cw=w_dw.shape[-1],
        w_a=w_a, w_u=w_u, w_q=w_q, w_kv=w_ukv[l].astype(BF),
        g_q=g_q[l][None], g_kv=g_kv[l][None],
        w_dw=w_dw[l], b_dw=b_dw[l][None], g_cn=g_cn[l][None], b_cn=b_cn[l][None],
        w_out=w_out[l].astype(BF), ln1_g=ln1_g[l][None], ln1_b=ln1_b[l][None],
        w_pq=w_pq[l].astype(BF), sub_keys=sub_keys[l].reshape(ph * two, nkeys, half).astype(BF),
        peer_u=peer_u[l].astype(BF), peer_v=peer_v[l].astype(BF),
        ln2_g=ln2_g[l][None], ln2_b=ln2_b[l][None],
    )


def kernel(x_prompt, x_sample, cache_ckv, cache_kpe, c, c_ctx, w_ada, b_ada, w_in, g_q, w_uq,
           g_kv, w_ukv, w_dw, b_dw, g_cn, b_cn, w_out, ln1_g, ln1_b, w_pq, sub_keys, peer_u,
           peer_v, ln2_g, ln2_b):
    depth = w_ada.shape[0]
    alpha = (2 * depth) ** 0.25
    B, T, D = x_prompt.shape
    Bd, Td, _ = x_sample.shape
    n_rows = -(-(1 + Bd) // SUBLANES) * SUBLANES
    cond = jnp.concatenate([c_ctx[None, :], c, jnp.zeros((n_rows - 1 - Bd, D), F32)], axis=0)

    n_tab = max(T, min(TOKEN_TILE, B * T))
    ones_tab = _lane_pad_table(jnp.ones((n_tab, ROPE_DIM), F32), jnp.zeros((n_tab, ROPE_DIM), F32))
    rope_tab = _rope_table(Td)

    xp, xs = x_prompt, x_sample
    ckv_layers, kpe_layers = [], []
    for l in range(depth):
        wts = _prep_weights(l, w_in, g_q, w_uq, g_kv, w_ukv, w_dw, b_dw, g_cn, b_cn, w_out,
                            ln1_g, ln1_b, w_pq, sub_keys, peer_u, peer_v, ln2_g, ln2_b)
        mod = _adaln(cond, w_ada[l], b_ada[l]).reshape(n_rows * 6, 1, D)
        xp, ckv, kpe = _path(xp, mod, lambda b: 0, True, ones_tab, wts, alpha)
        ckv_layers.append(ckv)
        kpe_layers.append(kpe)
        xs, _, _ = _path(xs, mod, lambda b: 1 + b, False, rope_tab, wts, alpha,
                         cache=(cache_ckv[:, l], cache_kpe[:, l]))
    return (xp, xs, jnp.stack(ckv_layers, axis=1), jnp.stack(kpe_layers, axis=1))
```

```python
import functools
import math

import jax
import jax.numpy as jnp
from jax import lax
from jax.experimental import pallas as pl
from jax.experimental.pallas import tpu as pltpu

F32 = jnp.float32
BF = jnp.bfloat16

NOPE_DIM = 128
ROPE_DIM = 64
V_DIM = 128
HEAD_Q = NOPE_DIM + 2 * ROPE_DIM
HEAD_KV = NOPE_DIM + V_DIM
GRID_W = 64
ROPE_BASE = 10000.0
PEER_TOPK = 16
EPS = 1e-6
LOG2_E = 1.4426950408889634

LANES = 128
SUBLANES = 8
VMEM_MIB = 1 << 20

TOKEN_TILE = 512
OUTPROJ_K_TILE = 512
PEER_TOKEN_TILE = 512
PEER_EXPERT_TILE = 1024
PEER_Q_GROUP = 4
DOT_HEAD_GROUP = 4
ATTN_Q_TILE = 256
CONV_T_TILE = 256
CONV_T_CHUNK = 64
CONV_C_CHUNK = 128
CONV_HALO = 16


def _cp(sem, vmem_mib):
    return pltpu.CompilerParams(dimension_semantics=sem, vmem_limit_bytes=vmem_mib * VMEM_MIB)


def _rope_lanes(x, tab):
    q = ROPE_DIM // 4
    lane = lax.broadcasted_iota(jnp.int32, x.shape, 1)
    x = jnp.where(lane < ROPE_DIM, x, 0.0)
    nxt = pltpu.roll(x, LANES - q, 1)
    prv = pltpu.roll(x, q, 1)
    rot = jnp.where((lane // q) % 2 == 0, -nxt, prv)
    return x * tab[:, :LANES] + rot * tab[:, LANES:]


def _layer_norm_rows(z, g, b):
    mu = jnp.mean(z, -1, keepdims=True)
    zc = z - mu
    var = jnp.mean(zc * zc, -1, keepdims=True)
    return zc * lax.rsqrt(var + EPS) * g + b


def _rms_norm_rows(z, g):
    return z * lax.rsqrt(jnp.mean(z * z, -1, keepdims=True) + EPS) * g


def _adaln_kernel(c_ref, w_ref, b_ref, o_ref):
    c = c_ref[...]
    s = (c * jax.nn.sigmoid(c)).astype(BF)
    o_ref[...] = jnp.dot(s, w_ref[...].astype(BF), preferred_element_type=F32) + b_ref[...]


def _adaln(cond, w, b):
    R, D = cond.shape
    N = w.shape[1]
    tn = min(512, N)
    return pl.pallas_call(
        _adaln_kernel,
        grid=(N // tn,),
        in_specs=[pl.BlockSpec((R, D), lambda j: (0, 0)),
                  pl.BlockSpec((D, tn), lambda j: (0, j)),
                  pl.BlockSpec((1, tn), lambda j: (0, j))],
        out_specs=pl.BlockSpec((R, tn), lambda j: (0, j)),
        out_shape=jax.ShapeDtypeStruct((R, N), F32),
        compiler_params=_cp(("arbitrary",), 40),
        name="adaln",
    )(cond, w, b.reshape(1, N))


def _inproj_a_kernel(x_ref, sc_ref, sh_ref, w_ref, gq_ref, gkv_ref, rope_ref,
                     qcn_ref, ckv_ref, kpe_ref, kpad_ref, h_ref, *, qr, kvr):
    h_ref[...] = (x_ref[...] * (1.0 + sc_ref[0]) + sh_ref[0]).astype(BF)
    p = jnp.dot(h_ref[...], w_ref[...], preferred_element_type=F32)
    qcn_ref[...] = _rms_norm_rows(p[:, :qr], gq_ref[...]).astype(BF)
    ckv_ref[...] = _rms_norm_rows(p[:, qr:qr + kvr], gkv_ref[...])
    kslab = p[:, qr + kvr:qr + kvr + LANES]
    kpe_ref[...] = kslab[:, :ROPE_DIM]
    kpad_ref[...] = _rope_lanes(kslab, rope_ref[...]).astype(BF)


def _inproj_u_kernel(h_ref, wa_ref, wg_ref, y_ref):
    h = h_ref[...]
    a = jnp.dot(h, wa_ref[...], preferred_element_type=F32)
    g = jnp.dot(h, wg_ref[...], preferred_element_type=F32)
    y_ref[...] = a * jax.nn.sigmoid(g)


def _qproj_kernel(x_ref, w_ref, rope_ref, o_ref, *, heads):
    x = x_ref[...]
    rope = rope_ref[...]
    group = math.gcd(heads, DOT_HEAD_GROUP)
    for g in range(heads // group):
        qg = jnp.dot(x, w_ref[:, g * group * HEAD_Q:(g + 1) * group * HEAD_Q],
                     preferred_element_type=F32)
        for j in range(group):
            b = (g * group + j) * HEAD_Q
            q = qg[:, j * HEAD_Q:(j + 1) * HEAD_Q]
            o_ref[:, b:b + NOPE_DIM] = q[:, :NOPE_DIM].astype(BF)
            o_ref[:, b + NOPE_DIM:b + HEAD_Q] = _rope_lanes(q[:, NOPE_DIM:], rope).astype(BF)


def _kvproj_kernel(x_ref, w_ref, o_ref, *, heads):
    x = x_ref[...].astype(BF)
    width = math.gcd(heads, DOT_HEAD_GROUP) * HEAD_KV
    for g in range(heads * HEAD_KV // width):
        o_ref[:, g * width:(g + 1) * width] = jnp.dot(
            x, w_ref[:, g * width:(g + 1) * width], preferred_element_type=F32).astype(BF)


def _attn_kernel(q_ref, kv_ref, kpad_ref, o_ref, *, heads, scale):
    kpad = kpad_ref[0]
    for h in range(heads):
        qh = q_ref[0, :, h * HEAD_Q:(h + 1) * HEAD_Q]
        kh = jnp.concatenate([kv_ref[0, :, h * HEAD_KV:h * HEAD_KV + NOPE_DIM], kpad], axis=1)
        s = lax.dot_general(qh, kh, (((1,), (1,)), ((), ())), preferred_element_type=F32)
        m = jnp.max(s, -1, keepdims=True)
        p = jnp.exp2((s - m) * (scale * LOG2_E))
        l = jnp.sum(p, -1, keepdims=True)
        o = jnp.dot(p.astype(BF), kv_ref[0, :, h * HEAD_KV + NOPE_DIM:(h + 1) * HEAD_KV],
                    preferred_element_type=F32)
        o_ref[0, :, h * V_DIM:(h + 1) * V_DIM] = (o / l).astype(BF)


def _conv_kernel(y_ref, w_ref, bdw_ref, g_ref, b_ref, o_ref, pad_scr, z_scr, *, tt, taps):
    t = pl.program_id(1)
    nt = pl.num_programs(1)
    C = z_scr.shape[1]
    half = taps // 2
    pad_scr[CONV_HALO:CONV_HALO + tt, :] = y_ref[0, pl.ds(pl.multiple_of(t * tt, tt), tt), :]

    @pl.when(t == 0)
    def _():
        pad_scr[0:CONV_HALO, :] = jnp.zeros((CONV_HALO, C), F32)

    @pl.when(t > 0)
    def _():
        pad_scr[0:CONV_HALO, :] = y_ref[
            0, pl.ds(pl.multiple_of(t * tt - CONV_HALO, SUBLANES), CONV_HALO), :]

    @pl.when(t == nt - 1)
    def _():
        pad_scr[CONV_HALO + tt:, :] = jnp.zeros((CONV_HALO, C), F32)

    @pl.when(t < nt - 1)
    def _():
        pad_scr[CONV_HALO + tt:, :] = y_ref[
            0, pl.ds(pl.multiple_of((t + 1) * tt, tt), CONV_HALO), :]

    tc = CONV_T_CHUNK
    cc = min(CONV_C_CHUNK, C)
    base = CONV_HALO - half
    wlen = tc + 2 * CONV_HALO
    assert base >= 0 and base + taps - 1 + tc <= wlen

    def chan_body(ci, carry):
        c0 = pl.multiple_of(ci * cc, cc)
        for r in range(tt // tc):
            win = pad_scr[r * tc:r * tc + wlen, pl.ds(c0, cc)]
            acc = jnp.zeros((tc, cc), F32)
            for sh in range(SUBLANES):
                rolled = win if sh == 0 else pltpu.roll(win, wlen - sh, 0)
                for a in range(wlen // SUBLANES):
                    k = a * SUBLANES + sh - base
                    if 0 <= k < taps:
                        acc = acc + (rolled[a * SUBLANES:a * SUBLANES + tc, :]
                                     * w_ref[k:k + 1, pl.ds(c0, cc)])
            z_scr[r * tc:(r + 1) * tc, pl.ds(c0, cc)] = acc + bdw_ref[:, pl.ds(c0, cc)]
        return carry

    lax.fori_loop(0, C // cc, chan_body, 0)

    rows = 64
    for r in range(tt // rows):
        zn = _layer_norm_rows(z_scr[r * rows:(r + 1) * rows, :], g_ref[...], b_ref[...])
        o_ref[0, r * rows:(r + 1) * rows, :] = (zn * jax.nn.sigmoid(zn)).astype(BF)


def _ln_inplace(xo_ref, x_ref, gate_ref, lg_ref, lb_ref, ff_ref, alpha):
    rows = 64
    for r in range(xo_ref.shape[0] // rows):
        sl = slice(r * rows, (r + 1) * rows)
        z = alpha * x_ref[sl, :] + gate_ref[0] * ff_ref[sl, :]
        xo_ref[sl, :] = _layer_norm_rows(z, lg_ref[...], lb_ref[...])


def _outproj_kernel(a_ref, c_ref, w_ref, x_ref, g1_ref, lg_ref, lb_ref, xo_ref, *, na, nk, alpha):
    k = pl.program_id(1)

    @pl.when(k == 0)
    def _():
        xo_ref[...] = jnp.zeros(xo_ref.shape, F32)

    def accumulate(lhs_ref):
        lhs = lhs_ref[...]
        nc = min(512, xo_ref.shape[1])
        for n in range(xo_ref.shape[1] // nc):
            cs = slice(n * nc, (n + 1) * nc)
            xo_ref[:, cs] += jnp.dot(lhs, w_ref[:, cs], preferred_element_type=F32)

    @pl.when(k < na)
    def _():
        accumulate(a_ref)

    @pl.when(k >= na)
    def _():
        accumulate(c_ref)

    @pl.when(k == nk - 1)
    def _():
        _ln_inplace(xo_ref, x_ref, g1_ref, lg_ref, lb_ref, xo_ref, alpha)


def _peer_q_kernel(x_ref, sc_ref, sh_ref, w_ref, k_ref, h_ref, s_ref):
    @pl.when(pl.program_id(1) == 0)
    def _():
        h_ref[...] = (x_ref[...] * (1.0 + sc_ref[0]) + sh_ref[0]).astype(BF)

    half = k_ref.shape[2]
    q = jnp.dot(h_ref[...], w_ref[...], preferred_element_type=F32).astype(BF)
    for p in range(k_ref.shape[0]):
        s_ref[p] = lax.dot_general(k_ref[p], q[:, p * half:(p + 1) * half],
                                   (((1,), (1,)), ((), ())), preferred_element_type=F32)


def _route_kernel(s_ref, st_ref, e2_ref, t2_scr, cand_scr):
    n_top = PEER_TOPK + 1
    neg = -jnp.inf
    tl = s_ref.shape[2]

    def top(p):
        n_grp = s_ref.shape[1] // SUBLANES
        col = [s_ref[p, g * SUBLANES:(g + 1) * SUBLANES, :] for g in range(n_grp)]
        for i, j in _sorting_network(n_grp):
            col[i], col[j] = jnp.maximum(col[i], col[j]), jnp.minimum(col[i], col[j])
        col.append(jnp.full(col[0].shape, neg, F32))
        out = []
        for it in range(n_top):
            m = jnp.max(col[0], axis=0, keepdims=True)
            out.append(m)
            took = col[0] == m
            for g in range(min(n_grp, n_top - 1 - it)):
                col[g] = jnp.where(took, col[g + 1], col[g])
        return out

    t1 = top(0)
    t2 = top(1)
    for k in range(n_top):
        t2_scr[k:k + 1, :] = t2[k]
    cand_scr[...] = jnp.full(cand_scr.shape, neg, F32)
    off = 0
    for a in range(n_top):
        nb = n_top // (a + 1)
        cand_scr[off:off + nb, :] = t2_scr[0:nb, :] + t1[a]
        off += nb
    cand = cand_scr[...]
    c = cand
    kth = None
    nxt = None
    for it in range(n_top):
        m = jnp.max(c, axis=0, keepdims=True)
        if it == PEER_TOPK - 1:
            kth = m
        if it == PEER_TOPK:
            nxt = m
        c = jnp.where(c == m, neg, c)
    tau = 0.5 * (kth + nxt)
    top_sum = t1[0] + t2[0]
    z = jnp.sum(jnp.where(cand > tau, jnp.exp(cand - top_sum), 0.0), axis=0, keepdims=True)
    st_ref[0, 0:1, :] = tau
    st_ref[0, 1:2, :] = t1[0]
    log_norm = t2[0] + jnp.log(z)
    st_ref[0, 2:3, :] = log_norm
    st_ref[0, 3:, :] = jnp.zeros((SUBLANES - 3, tl), F32)
    e2_ref[0] = jnp.exp(s_ref[1] - log_norm)


def _sorting_network(n):
    assert n & (n - 1) == 0

    def merge(lo, hi, r):
        step = r * 2
        if step < hi - lo:
            yield from merge(lo, hi, step)
            yield from merge(lo + r, hi, step)
            yield from ((i, i + r) for i in range(lo + r, hi - r, step))
        else:
            yield (lo, lo + r)

    def sort(lo, hi):
        if hi - lo >= 1:
            mid = lo + (hi - lo) // 2
            yield from sort(lo, mid)
            yield from sort(mid + 1, hi)
            yield from merge(lo, hi, 1)

    return list(sort(0, n - 1))


def _n_cand(n_top):
    return sum(n_top // (a + 1) for a in range(n_top))


def _gelu_exact(x):
    return 0.5 * x * (1.0 + lax.erf(x * (1.0 / math.sqrt(2.0))))


def _peer_kernel(h_ref, s1_ref, e2_ref, st_ref, u_ref, v_ref, o_ref, a_scr, w_scr, *, heads, keys):
    nt = h_ref.shape[0]
    et = u_ref.shape[0]
    n_halves = 2 if nt >= 2 * LANES else 1
    half = nt // n_halves

    @pl.when(pl.program_id(1) == 0)
    def _():
        o_ref[...] = jnp.zeros(o_ref.shape, F32)

    def weight_block(ii, c):
        cols = slice(c * LANES, (c + 1) * LANES)
        w = jnp.zeros((keys, LANES), F32)
        for h in range(heads):
            s1 = s1_ref[h, ii:ii + 1, cols]
            e2_min = jnp.exp(st_ref[h, 0:1, cols] - s1 - st_ref[h, 2:3, cols])
            e1 = jnp.exp(s1 - st_ref[h, 1:2, cols])
            e2 = e2_ref[h, :, cols]
            w = w + jnp.where(e2 >= e2_min, e2, 0.0) * e1
        ars = slice(ii * keys, (ii + 1) * keys)
        w_scr[cols, ars] = (w * _gelu_exact(a_scr[ars, cols])).T.astype(BF)

    def up_rows(k):
        rs = slice(k * half, (k + 1) * half)
        w_rows = w_scr[rs, :]
        nc = min(512, o_ref.shape[1])
        for n in range(o_ref.shape[1] // nc):
            cs = slice(n * nc, (n + 1) * nc)
            o_ref[rs, cs] += jnp.dot(w_rows, v_ref[:, cs], preferred_element_type=F32)

    def down_cols(k):
        cols = slice(k * half, (k + 1) * half)
        a_scr[:, cols] = lax.dot_general(u_ref[...], h_ref[cols, :], (((1,), (1,)), ((), ())),
                                         preferred_element_type=F32)

    for k in range(n_halves):
        down_cols(k)
    for k in range(n_halves):
        for ii in range(et // keys):
            for c in range(k * half // LANES, (k + 1) * half // LANES):
                weight_block(ii, c)
        up_rows(k)


def _ln2_kernel(x_ref, ff_ref, g2_ref, lg_ref, lb_ref, o_ref, *, alpha):
    _ln_inplace(o_ref, x_ref, g2_ref, lg_ref, lb_ref, ff_ref, alpha)


def _mod_spec(D, which, row_fn):
    return pl.BlockSpec((1, 1, D), lambda i, *_: (row_fn(i) * 6 + which, 0, 0))


def _path(x, mod, row_of_seq, shared_mod, rope_tab, wts, alpha, cache=None):
    B, T, D = x.shape
    M = B * T
    x2 = x.reshape(M, D)
    heads = wts["heads"]
    qr, kvr = wts["qr"], wts["kvr"]
    CW = wts["cw"]
    MW = heads * V_DIM
    tm = min(TOKEN_TILE, M) if shared_mod else min(TOKEN_TILE, T)
    assert M % tm == 0 and (T % tm == 0 or tm % T == 0)
    assert rope_tab.shape[0] == max(T, tm)

    def row_fn(i):
        return row_of_seq((i * tm) // T)

    def rope_idx(i, *_):
        return (i % max(T // tm, 1), 0)

    mspec = functools.partial(_mod_spec, D, row_fn=row_fn)
    row_spec = pl.BlockSpec((tm, D), lambda i, *_: (i, 0))

    wa = wts["w_a"]
    NA = wa.shape[1]
    qcn, ckv, kpe, kpad, h1 = pl.pallas_call(
        functools.partial(_inproj_a_kernel, qr=qr, kvr=kvr),
        grid=(M // tm,),
        in_specs=[row_spec, mspec(1), mspec(0),
                  pl.BlockSpec((D, NA), lambda i: (0, 0), pipeline_mode=pl.Buffered(1)),
                  pl.BlockSpec((1, qr), lambda i: (0, 0)),
                  pl.BlockSpec((1, kvr), lambda i: (0, 0)),
                  pl.BlockSpec((tm, 2 * LANES), rope_idx)],
        out_specs=[pl.BlockSpec((tm, qr), lambda i: (i, 0)),
                   pl.BlockSpec((tm, kvr), lambda i: (i, 0)),
                   pl.BlockSpec((tm, ROPE_DIM), lambda i: (i, 0)),
                   pl.BlockSpec((tm, LANES), lambda i: (i, 0)),
                   row_spec],
        out_shape=[jax.ShapeDtypeStruct((M, qr), BF),
                   jax.ShapeDtypeStruct((M, kvr), F32),
                   jax.ShapeDtypeStruct((M, ROPE_DIM), F32),
                   jax.ShapeDtypeStruct((M, LANES), BF),
                   jax.ShapeDtypeStruct((M, D), BF)],
        compiler_params=_cp(("arbitrary",), 56),
        name="inproj_a",
    )(x2, mod, mod, wa, wts["g_q"], wts["g_kv"], rope_tab)

    wu = wts["w_u"]
    tn = min(512, CW)
    y = pl.pallas_call(
        _inproj_u_kernel,
        grid=(M // tm, CW // tn),
        in_specs=[row_spec,
                  pl.BlockSpec((D, tn), lambda i, j: (0, j)),
                  pl.BlockSpec((D, tn), lambda i, j: (0, j + CW // tn))],
        out_specs=pl.BlockSpec((tm, tn), lambda i, j: (i, j)),
        out_shape=jax.ShapeDtypeStruct((M, CW), F32),
        compiler_params=_cp(("arbitrary", "arbitrary"), 48),
        name="inproj_u",
    )(h1, wu, wu)

    wq = wts["w_q"]
    q = pl.pallas_call(
        functools.partial(_qproj_kernel, heads=heads),
        grid=(M // tm,),
        in_specs=[pl.BlockSpec((tm, qr), lambda i: (i, 0)),
                  pl.BlockSpec(wq.shape, lambda i: (0, 0)),
                  pl.BlockSpec((tm, 2 * LANES), rope_idx)],
        out_specs=pl.BlockSpec((tm, heads * HEAD_Q), lambda i: (i, 0)),
        out_shape=jax.ShapeDtypeStruct((M, heads * HEAD_Q), BF),
        compiler_params=_cp(("arbitrary",), 48),
        name="qproj",
    )(qcn, wq, rope_tab)

    if cache is not None:
        cache_ckv, cache_kpe = cache
        P = cache_ckv.shape[1]
        ckv_all = jnp.concatenate([cache_ckv, ckv.reshape(B, T, kvr)], axis=1)
        kpad_all = jnp.concatenate(
            [jnp.pad(cache_kpe, ((0, 0), (0, 0), (0, LANES - ROPE_DIM))).astype(BF),
             kpad.reshape(B, T, LANES)], axis=1)
    else:
        P = 0
        ckv_all = ckv.reshape(B, T, kvr)
        kpad_all = kpad.reshape(B, T, LANES)
    S = P + T
    MS = B * S
    tk = math.gcd(MS, TOKEN_TILE)
    wkv = wts["w_kv"]
    kv = pl.pallas_call(
        functools.partial(_kvproj_kernel, heads=heads),
        grid=(MS // tk,),
        in_specs=[pl.BlockSpec((tk, kvr), lambda i: (i, 0)),
                  pl.BlockSpec(wkv.shape, lambda i: (0, 0))],
        out_specs=pl.BlockSpec((tk, heads * HEAD_KV), lambda i: (i, 0)),
        out_shape=jax.ShapeDtypeStruct((MS, heads * HEAD_KV), BF),
        compiler_params=_cp(("arbitrary",), 40),
        name="kvproj",
    )(ckv_all.reshape(MS, kvr), wkv)

    tq = min(ATTN_Q_TILE, T)
    attn = pl.pallas_call(
        functools.partial(_attn_kernel, heads=heads, scale=(NOPE_DIM + ROPE_DIM) ** -0.5),
        grid=(B, T // tq),
        in_specs=[pl.BlockSpec((1, tq, heads * HEAD_Q), lambda b, t: (b, t, 0)),
                  pl.BlockSpec((1, S, heads * HEAD_KV), lambda b, t: (b, 0, 0)),
                  pl.BlockSpec((1, S, LANES), lambda b, t: (b, 0, 0))],
        out_specs=pl.BlockSpec((1, tq, MW), lambda b, t: (b, t, 0)),
        out_shape=jax.ShapeDtypeStruct((B, T, MW), BF),
        compiler_params=_cp(("arbitrary", "arbitrary"), 48),
        name="attn",
    )(q.reshape(B, T, heads * HEAD_Q), kv.reshape(B, S, heads * HEAD_KV), kpad_all)

    tt = min(CONV_T_TILE, T)
    taps = wts["w_dw"].shape[0]
    conv = pl.pallas_call(
        functools.partial(_conv_kernel, tt=tt, taps=taps),
        grid=(B, T // tt),
        in_specs=[pl.BlockSpec((1, T, CW), lambda b, t: (b, 0, 0)),
                  pl.BlockSpec((taps, CW), lambda b, t: (0, 0)),
                  pl.BlockSpec((1, CW), lambda b, t: (0, 0)),
                  pl.BlockSpec((1, CW), lambda b, t: (0, 0)),
                  pl.BlockSpec((1, CW), lambda b, t: (0, 0))],
        out_specs=pl.BlockSpec((1, tt, CW), lambda b, t: (b, t, 0)),
        out_shape=jax.ShapeDtypeStruct((B, T, CW), BF),
        scratch_shapes=[pltpu.VMEM((tt + 2 * CONV_HALO, CW), F32), pltpu.VMEM((tt, CW), F32)],
        compiler_params=_cp(("arbitrary", "arbitrary"), 40),
        name="conv",
    )(y.reshape(B, T, CW), wts["w_dw"], wts["b_dw"], wts["g_cn"], wts["b_cn"])

    wo = wts["w_out"]
    tkk = min(OUTPROJ_K_TILE, MW)
    na = MW // tkk
    nk = na + CW // tkk
    vec_spec = pl.BlockSpec((1, D), lambda i, *_: (0, 0))
    x1 = pl.pallas_call(
        functools.partial(_outproj_kernel, na=na, nk=nk, alpha=alpha),
        grid=(M // tm, nk),
        in_specs=[pl.BlockSpec((tm, tkk), lambda i, k: (i, jnp.minimum(k, na - 1))),
                  pl.BlockSpec((tm, tkk), lambda i, k: (i, jnp.maximum(k - na, 0))),
                  pl.BlockSpec((tkk, D), lambda i, k: (k, 0)),
                  row_spec, mspec(2), vec_spec, vec_spec],
        out_specs=row_spec,
        out_shape=jax.ShapeDtypeStruct((M, D), F32),
        compiler_params=_cp(("arbitrary", "arbitrary"), 56),
        name="outproj",
    )(attn.reshape(M, MW), conv.reshape(M, CW), wo, x2, mod, wts["ln1_g"], wts["ln1_b"])

    wpq = wts["w_pq"]
    keys_bf = wts["sub_keys"]
    nhp, nkeys, half = keys_bf.shape
    ph = nhp // 2
    gq = math.gcd(nhp, PEER_Q_GROUP)
    h2, sT = pl.pallas_call(
        _peer_q_kernel,
        grid=(M // tm, nhp // gq),
        in_specs=[row_spec, mspec(4), mspec(3),
                  pl.BlockSpec((D, gq * half), lambda i, j: (0, j)),
                  pl.BlockSpec((gq, nkeys, half), lambda i, j: (j, 0, 0))],
        out_specs=[pl.BlockSpec((tm, D), lambda i, j: (i, 0)),
                   pl.BlockSpec((gq, nkeys, tm), lambda i, j: (j, 0, i))],
        out_shape=[jax.ShapeDtypeStruct((M, D), BF),
                   jax.ShapeDtypeStruct((nhp, nkeys, M), F32)],
        compiler_params=_cp(("arbitrary", "arbitrary"), 48),
        name="peer_q",
    )(x1, mod, mod, wpq, keys_bf)

    tl = min(256, M)
    stats, e2 = pl.pallas_call(
        _route_kernel,
        grid=(M // tl, ph),
        in_specs=[pl.BlockSpec((2, nkeys, tl), lambda i, h: (h, 0, i))],
        out_specs=[pl.BlockSpec((1, SUBLANES, tl), lambda i, h: (h, 0, i)),
                   pl.BlockSpec((1, nkeys, tl), lambda i, h: (h, 0, i))],
        out_shape=[jax.ShapeDtypeStruct((ph, SUBLANES, M), F32),
                   jax.ShapeDtypeStruct((ph, nkeys, M), F32)],
        scratch_shapes=[pltpu.VMEM((24, tl), F32),
                        pltpu.VMEM((-(-_n_cand(PEER_TOPK + 1) // SUBLANES) * SUBLANES, tl), F32)],
        compiler_params=_cp(("arbitrary", "arbitrary"), 32),
        name="peer_route",
    )(sT)

    pu, pv = wts["peer_u"], wts["peer_v"]
    NE = pu.shape[0]
    nt = min(PEER_TOKEN_TILE, M)
    et = min(PEER_EXPERT_TILE, NE)
    rows = et // nkeys
    assert et % nkeys == 0 and NE == nkeys * nkeys and rows % SUBLANES == 0
    ff = pl.pallas_call(
        functools.partial(_peer_kernel, heads=ph, keys=nkeys),
        grid=(M // nt, NE // et),
        in_specs=[pl.BlockSpec((nt, D), lambda i, e: (i, 0)),
                  pl.BlockSpec((ph, None, rows, nt), lambda i, e: (0, 0, e, i)),
                  pl.BlockSpec((ph, nkeys, nt), lambda i, e: (0, 0, i)),
                  pl.BlockSpec((ph, SUBLANES, nt), lambda i, e: (0, 0, i)),
                  pl.BlockSpec((et, D), lambda i, e: (e, 0)),
                  pl.BlockSpec((et, D), lambda i, e: (e, 0))],
        out_specs=pl.BlockSpec((nt, D), lambda i, e: (i, 0), pipeline_mode=pl.Buffered(1)),
        out_shape=jax.ShapeDtypeStruct((M, D), F32),
        scratch_shapes=[pltpu.VMEM((et, nt), F32),
                        pltpu.VMEM((nt, et), BF)],
        compiler_params=_cp(("arbitrary", "arbitrary"), 60),
        name="peer_main",
    )(h2, sT.reshape(ph, 2, nkeys, M), e2, stats, pu, pv)

    t2 = min(256, T)

    def row_fn2(i):
        return row_of_seq((i * t2) // T)

    row2 = pl.BlockSpec((t2, D), lambda i: (i, 0))
    y_out = pl.pallas_call(
        functools.partial(_ln2_kernel, alpha=alpha),
        grid=(M // t2,),
        in_specs=[row2, row2, _mod_spec(D, 5, row_fn2),
                  pl.BlockSpec((1, D), lambda i: (0, 0)), pl.BlockSpec((1, D), lambda i: (0, 0))],
        out_specs=row2,
        out_shape=jax.ShapeDtypeStruct((M, D), F32),
        compiler_params=_cp(("arbitrary",), 40),
        name="ln2",
    )(x1, ff, mod, wts["ln2_g"], wts["ln2_b"])

    return y_out.reshape(B, T, D), ckv.reshape(B, T, kvr), kpe.reshape(B, T, ROPE_DIM)


def _rope_table(n_tokens):
    rows = n_tokens // GRID_W
    row = jnp.repeat(jnp.arange(rows, dtype=F32), GRID_W)
    col = jnp.tile(jnp.arange(GRID_W, dtype=F32), rows)
    n_freq = ROPE_DIM // 4
    inv = ROPE_BASE ** (-jnp.arange(n_freq, dtype=F32) / n_freq)
    ang_r = row[:, None] * inv
    ang_c = col[:, None] * inv
    ang = jnp.concatenate([ang_r, ang_r, ang_c, ang_c], -1)
    return _lane_pad_table(jnp.cos(ang), jnp.sin(ang))


def _lane_pad_table(cos, sin):
    zero = jnp.zeros((cos.shape[0], LANES - ROPE_DIM), F32)
    return jnp.concatenate([cos, zero, sin, zero], -1)


def _prep_weights(l, w_in, g_q, w_uq, g_kv, w_ukv, w_dw, b_dw, g_cn, b_cn, w_out,
                  ln1_g, ln1_b, w_pq, sub_keys, peer_u, peer_v, ln2_g, ln2_b):
    qr = g_q.shape[-1]
    kvr = g_kv.shape[-1]
    o3 = qr + kvr + ROPE_DIM
    heads = w_uq.shape[-1] // (NOPE_DIM + ROPE_DIM)
    wi = w_in[l]
    n_a = qr + kvr + LANES
    assert (qr + kvr) % LANES == 0 and n_a <= wi.shape[1]
    wi_bf = wi.astype(BF)
    w_a = wi_bf[:, :n_a]
    w_u = wi_bf[:, o3:]
    wq3 = w_uq[l].reshape(qr, heads, NOPE_DIM + ROPE_DIM).astype(BF)
    w_q = jnp.pad(wq3, ((0, 0), (0, 0), (0, HEAD_Q - NOPE_DIM - ROPE_DIM)))
    w_q = w_q.reshape(qr, heads * HEAD_Q)
    ph, two, nkeys, half = sub_keys.shape[1:]
    return dict(
        heads=heads, qr=qr, kvr=kvr, cw=w_dw.shape[-1],
        w_a=w_a, w_u=w_u, w_q=w_q, w_kv=w_ukv[l].astype(BF),
        g_q=g_q[l][None], g_kv=g_kv[l][None],
        w_dw=w_dw[l], b_dw=b_dw[l][None], g_cn=g_cn[l][None], b_cn=b_cn[l][None],
        w_out=w_out[l].astype(BF), ln1_g=ln1_g[l][None], ln1_b=ln1_b[l][None],
        w_pq=w_pq[l].astype(BF), sub_keys=sub_keys[l].reshape(ph * two, nkeys, half).astype(BF),
        peer_u=peer_u[l].astype(BF), peer_v=peer_v[l].astype(BF),
        ln2_g=ln2_g[l][None], ln2_b=ln2_b[l][None],
    )


def kernel(x_prompt, x_sample, cache_ckv, cache_kpe, c, c_ctx, w_ada, b_ada, w_in, g_q, w_uq,
           g_kv, w_ukv, w_dw, b_dw, g_cn, b_cn, w_out, ln1_g, ln1_b, w_pq, sub_keys, peer_u,
           peer_v, ln2_g, ln2_b):
    depth = w_ada.shape[0]
    alpha = (2 * depth) ** 0.25
    B, T, D = x_prompt.shape
    Bd, Td, _ = x_sample.shape
    n_rows = -(-(1 + Bd) // SUBLANES) * SUBLANES
    cond = jnp.concatenate([c_ctx[None, :], c, jnp.zeros((n_rows - 1 - Bd, D), F32)], axis=0)

    n_tab = max(T, min(TOKEN_TILE, B * T))
    ones_tab = _lane_pad_table(jnp.ones((n_tab, ROPE_DIM), F32), jnp.zeros((n_tab, ROPE_DIM), F32))
    rope_tab = _rope_table(Td)

    xp, xs = x_prompt, x_sample
    ckv_layers, kpe_layers = [], []
    for l in range(depth):
        wts = _prep_weights(l, w_in, g_q, w_uq, g_kv, w_ukv, w_dw, b_dw, g_cn, b_cn, w_out,
                            ln1_g, ln1_b, w_pq, sub_keys, peer_u, peer_v, ln2_g, ln2_b)
        mod = _adaln(cond, w_ada[l], b_ada[l]).reshape(n_rows * 6, 1, D)
        xp, ckv, kpe = _path(xp, mod, lambda b: 0, True, ones_tab, wts, alpha)
        ckv_layers.append(ckv)
        kpe_layers.append(kpe)
        xs, _, _ = _path(xs, mod, lambda b: 1 + b, False, rope_tab, wts, alpha,
                         cache=(cache_ckv[:, l], cache_kpe[:, l]))
    return (xp, xs, jnp.stack(ckv_layers, axis=1), jnp.stack(kpe_layers, axis=1))
```

```python
import functools
import math

import jax
import jax.numpy as jnp
from jax import lax
from jax.experimental import pallas as pl
from jax.experimental.pallas import tpu as pltpu

F32 = jnp.float32
BF = jnp.bfloat16

NOPE_DIM = 128
ROPE_DIM = 64
V_DIM = 128
HEAD_Q = NOPE_DIM + 2 * ROPE_DIM
HEAD_KV = NOPE_DIM + V_DIM
GRID_W = 64
ROPE_BASE = 10000.0
PEER_TOPK = 16
EPS = 1e-6
LOG2_E = 1.4426950408889634

LANES = 128
SUBLANES = 8
VMEM_MIB = 1 << 20

TOKEN_TILE = 512
GLU_TOKEN_TILE = 1024
OUTPROJ_N_TILE = 512
PEER_TOKEN_TILE = 512
PEER_EXPERT_TILE = 1024
PEER_Q_GROUP = 4
DOT_HEAD_GROUP = 4
ATTN_Q_TILE = 256
CONV_T_TILE = 256
CONV_T_CHUNK = 64
CONV_C_CHUNK = 128
CONV_HALO = 16


def _cp(sem, vmem_mib):
    return pltpu.CompilerParams(dimension_semantics=sem, vmem_limit_bytes=vmem_mib * VMEM_MIB)


def _rope_lanes(x, tab):
    q = ROPE_DIM // 4
    lane = lax.broadcasted_iota(jnp.int32, x.shape, 1)
    x = jnp.where(lane < ROPE_DIM, x, 0.0)
    nxt = pltpu.roll(x, LANES - q, 1)
    prv = pltpu.roll(x, q, 1)
    rot = jnp.where((lane // q) % 2 == 0, -nxt, prv)
    return x * tab[:, :LANES] + rot * tab[:, LANES:]


def _layer_norm_rows(z, g, b):
    mu = jnp.mean(z, -1, keepdims=True)
    zc = z - mu
    var = jnp.mean(zc * zc, -1, keepdims=True)
    return zc * lax.rsqrt(var + EPS) * g + b


def _rms_norm_rows(z, g):
    return z * lax.rsqrt(jnp.mean(z * z, -1, keepdims=True) + EPS) * g


def _adaln_kernel(c_ref, w_ref, b_ref, o_ref):
    c = c_ref[...]
    s = (c * jax.nn.sigmoid(c)).astype(BF)
    o_ref[...] = jnp.dot(s, w_ref[...].astype(BF), preferred_element_type=F32) + b_ref[...]


def _adaln(cond, w, b):
    R, D = cond.shape
    N = w.shape[1]
    tn = min(512, N)
    return pl.pallas_call(
        _adaln_kernel,
        grid=(N // tn,),
        in_specs=[pl.BlockSpec((R, D), lambda j: (0, 0)),
                  pl.BlockSpec((D, tn), lambda j: (0, j)),
                  pl.BlockSpec((1, tn), lambda j: (0, j))],
        out_specs=pl.BlockSpec((R, tn), lambda j: (0, j)),
        out_shape=jax.ShapeDtypeStruct((R, N), F32),
        compiler_params=_cp(("arbitrary",), 40),
        name="adaln",
    )(cond, w, b.reshape(1, N))


def _inproj_a_kernel(x_ref, sc_ref, sh_ref, w_ref, gq_ref, gkv_ref, rope_ref,
                     qcn_ref, ckv_ref, kpe_ref, kpad_ref, h_ref, *, qr, kvr):
    h_ref[...] = (x_ref[...] * (1.0 + sc_ref[0]) + sh_ref[0]).astype(BF)
    p = jnp.dot(h_ref[...], w_ref[...], preferred_element_type=F32)
    qcn_ref[...] = _rms_norm_rows(p[:, :qr], gq_ref[...]).astype(BF)
    ckv_ref[...] = _rms_norm_rows(p[:, qr:qr + kvr], gkv_ref[...])
    kslab = p[:, qr + kvr:qr + kvr + LANES]
    kpe_ref[...] = kslab[:, :ROPE_DIM]
    kpad_ref[...] = _rope_lanes(kslab, rope_ref[...]).astype(BF)


def _inproj_u_kernel(h_ref, wa_ref, wg_ref, y_ref):
    h = h_ref[...]
    a = jnp.dot(h, wa_ref[...], preferred_element_type=F32)
    g = jnp.dot(h, wg_ref[...], preferred_element_type=F32)
    y_ref[...] = a * jax.nn.sigmoid(g)


def _qproj_kernel(x_ref, w_ref, rope_ref, o_ref, *, heads):
    x = x_ref[...]
    rope = rope_ref[...]
    group = math.gcd(heads, DOT_HEAD_GROUP)
    for g in range(heads // group):
        qg = jnp.dot(x, w_ref[:, g * group * HEAD_Q:(g + 1) * group * HEAD_Q],
                     preferred_element_type=F32)
        for j in range(group):
            b = (g * group + j) * HEAD_Q
            q = qg[:, j * HEAD_Q:(j + 1) * HEAD_Q]
            o_ref[:, b:b + NOPE_DIM] = q[:, :NOPE_DIM].astype(BF)
            o_ref[:, b + NOPE_DIM:b + HEAD_Q] = _rope_lanes(q[:, NOPE_DIM:], rope).astype(BF)


def _kvproj_kernel(x_ref, w_ref, o_ref, *, heads):
    x = x_ref[...].astype(BF)
    width = math.gcd(heads, DOT_HEAD_GROUP) * HEAD_KV
    for g in range(heads * HEAD_KV // width):
        o_ref[:, g * width:(g + 1) * width] = jnp.dot(
            x, w_ref[:, g * width:(g + 1) * width], preferred_element_type=F32).astype(BF)


def _attn_kernel(q_ref, kv_ref, kpad_ref, o_ref, *, heads, scale):
    kpad = kpad_ref[0]
    for h in range(heads):
        qh = q_ref[0, :, h * HEAD_Q:(h + 1) * HEAD_Q]
        kh = jnp.concatenate([kv_ref[0, :, h * HEAD_KV:h * HEAD_KV + NOPE_DIM], kpad], axis=1)
        s = lax.dot_general(qh, kh, (((1,), (1,)), ((), ())), preferred_element_type=F32)
        m = jnp.max(s, -1, keepdims=True)
        p = jnp.exp2((s - m) * (scale * LOG2_E))
        l = jnp.sum(p, -1, keepdims=True)
        o = jnp.dot(p.astype(BF), kv_ref[0, :, h * HEAD_KV + NOPE_DIM:(h + 1) * HEAD_KV],
                    preferred_element_type=F32)
        o_ref[0, :, h * V_DIM:(h + 1) * V_DIM] = (o / l).astype(BF)


def _conv_kernel(y_ref, w_ref, bdw_ref, g_ref, b_ref, o_ref, pad_scr, z_scr, *, tt, taps):
    t = pl.program_id(1)
    nt = pl.num_programs(1)
    C = z_scr.shape[1]
    half = taps // 2
    pad_scr[CONV_HALO:CONV_HALO + tt, :] = y_ref[0, pl.ds(pl.multiple_of(t * tt, tt), tt), :]

    @pl.when(t == 0)
    def _():
        pad_scr[0:CONV_HALO, :] = jnp.zeros((CONV_HALO, C), F32)

    @pl.when(t > 0)
    def _():
        pad_scr[0:CONV_HALO, :] = y_ref[
            0, pl.ds(pl.multiple_of(t * tt - CONV_HALO, SUBLANES), CONV_HALO), :]

    @pl.when(t == nt - 1)
    def _():
        pad_scr[CONV_HALO + tt:, :] = jnp.zeros((CONV_HALO, C), F32)

    @pl.when(t < nt - 1)
    def _():
        pad_scr[CONV_HALO + tt:, :] = y_ref[
            0, pl.ds(pl.multiple_of((t + 1) * tt, tt), CONV_HALO), :]

    tc = CONV_T_CHUNK
    cc = min(CONV_C_CHUNK, C)
    base = CONV_HALO - half
    wlen = tc + 2 * CONV_HALO
    assert base >= 0 and base + taps - 1 + tc <= wlen

    def chan_body(ci, carry):
        c0 = pl.multiple_of(ci * cc, cc)
        for r in range(tt // tc):
            win = pad_scr[r * tc:r * tc + wlen, pl.ds(c0, cc)]
            acc = jnp.zeros((tc, cc), F32)
            for sh in range(SUBLANES):
                rolled = win if sh == 0 else pltpu.roll(win, wlen - sh, 0)
                for a in range(wlen // SUBLANES):
                    k = a * SUBLANES + sh - base
                    if 0 <= k < taps:
                        acc = acc + (rolled[a * SUBLANES:a * SUBLANES + tc, :]
                                     * w_ref[k:k + 1, pl.ds(c0, cc)])
            z_scr[r * tc:(r + 1) * tc, pl.ds(c0, cc)] = acc + bdw_ref[:, pl.ds(c0, cc)]
        return carry

    lax.fori_loop(0, C // cc, chan_body, 0)

    rows = 64
    for r in range(tt // rows):
        zn = _layer_norm_rows(z_scr[r * rows:(r + 1) * rows, :], g_ref[...], b_ref[...])
        o_ref[0, r * rows:(r + 1) * rows, :] = (zn * jax.nn.sigmoid(zn)).astype(BF)


def _ln_inplace(xo_ref, x_ref, gate_ref, lg_ref, lb_ref, ff_ref, alpha):
    rows = 64
    for r in range(xo_ref.shape[0] // rows):
        sl = slice(r * rows, (r + 1) * rows)
        z = alpha * x_ref[sl, :] + gate_ref[0] * ff_ref[sl, :]
        xo_ref[sl, :] = _layer_norm_rows(z, lg_ref[...], lb_ref[...])


def _outproj_kernel(a_ref, c_ref, wa_ref, wc_ref, x_ref, g1_ref, lg_ref, lb_ref, xo_ref,
                    *, tn, alpha):
    j = pl.program_id(1)
    col = pl.multiple_of(j * tn, tn)
    xo_ref[:, pl.ds(col, tn)] = (
        jnp.dot(a_ref[...], wa_ref[...], preferred_element_type=F32)
        + jnp.dot(c_ref[...], wc_ref[...], preferred_element_type=F32))

    @pl.when(j == pl.num_programs(1) - 1)
    def _():
        _ln_inplace(xo_ref, x_ref, g1_ref, lg_ref, lb_ref, xo_ref, alpha)


def _peer_q_kernel(x_ref, sc_ref, sh_ref, w_ref, k_ref, h_ref, s_ref):
    @pl.when(pl.program_id(1) == 0)
    def _():
        h_ref[...] = (x_ref[...] * (1.0 + sc_ref[0]) + sh_ref[0]).astype(BF)

    half = k_ref.shape[2]
    q = jnp.dot(h_ref[...], w_ref[...], preferred_element_type=F32).astype(BF)
    for p in range(k_ref.shape[0]):
        s_ref[p] = lax.dot_general(k_ref[p], q[:, p * half:(p + 1) * half],
                                   (((1,), (1,)), ((), ())), preferred_element_type=F32)


def _route_kernel(s_ref, st_ref, e2_ref, t2_scr, cand_scr):
    n_top = PEER_TOPK + 1
    neg = -jnp.inf
    tl = s_ref.shape[2]

    def top(p):
        n_grp = s_ref.shape[1] // SUBLANES
        col = [s_ref[p, g * SUBLANES:(g + 1) * SUBLANES, :] for g in range(n_grp)]
        for i, j in _sorting_network(n_grp):
            col[i], col[j] = jnp.maximum(col[i], col[j]), jnp.minimum(col[i], col[j])
        col.append(jnp.full(col[0].shape, neg, F32))
        out = []
        for it in range(n_top):
            m = jnp.max(col[0], axis=0, keepdims=True)
            out.append(m)
            took = col[0] == m
            for g in range(min(n_grp, n_top - 1 - it)):
                col[g] = jnp.where(took, col[g + 1], col[g])
        return out

    t1 = top(0)
    t2 = top(1)
    for k in range(n_top):
        t2_scr[k:k + 1, :] = t2[k]
    cand_scr[...] = jnp.full(cand_scr.shape, neg, F32)
    off = 0
    for a in range(n_top):
        nb = n_top // (a + 1)
        cand_scr[off:off + nb, :] = t2_scr[0:nb, :] + t1[a]
        off += nb
    cand = cand_scr[...]
    c = cand
    kth = None
    nxt = None
    for it in range(n_top):
        m = jnp.max(c, axis=0, keepdims=True)
        if it == PEER_TOPK - 1:
            kth = m
        if it == PEER_TOPK:
            nxt = m
        c = jnp.where(c == m, neg, c)
    tau = 0.5 * (kth + nxt)
    top_sum = t1[0] + t2[0]
    z = jnp.sum(jnp.where(cand > tau, jnp.exp(cand - top_sum), 0.0), axis=0, keepdims=True)
    st_ref[0, 0:1, :] = tau
    st_ref[0, 1:2, :] = t1[0]
    log_norm = t2[0] + jnp.log(z)
    st_ref[0, 2:3, :] = log_norm
    st_ref[0, 3:, :] = jnp.zeros((SUBLANES - 3, tl), F32)
    e2_ref[0] = jnp.exp(s_ref[1] - log_norm)


def _sorting_network(n):
    assert n & (n - 1) == 0

    def merge(lo, hi, r):
        step = r * 2
        if step < hi - lo:
            yield from merge(lo, hi, step)
            yield from merge(lo + r, hi, step)
            yield from ((i, i + r) for i in range(lo + r, hi - r, step))
        else:
            yield (lo, lo + r)

    def sort(lo, hi):
        if hi - lo >= 1:
            mid = lo + (hi - lo) // 2
            yield from sort(lo, mid)
            yield from sort(mid + 1, hi)
            yield from merge(lo, hi, 1)

    return list(sort(0, n - 1))


def _n_cand(n_top):
    return sum(n_top // (a + 1) for a in range(n_top))


def _gelu_exact(x):
    return 0.5 * x * (1.0 + lax.erf(x * (1.0 / math.sqrt(2.0))))


def _peer_kernel(h_ref, s1_ref, e2_ref, st_ref, u_ref, v_ref, o_ref, a_scr, w_scr, *, heads, keys):
    nt = h_ref.shape[0]
    et = u_ref.shape[0]
    n_halves = 2 if nt >= 2 * LANES else 1
    half = nt // n_halves

    @pl.when(pl.program_id(1) == 0)
    def _():
        o_ref[...] = jnp.zeros(o_ref.shape, F32)

    def weight_block(ii, c):
        cols = slice(c * LANES, (c + 1) * LANES)
        w = jnp.zeros((keys, LANES), F32)
        for h in range(heads):
            s1 = s1_ref[h, ii:ii + 1, cols]
            e2_min = jnp.exp(st_ref[h, 0:1, cols] - s1 - st_ref[h, 2:3, cols])
            e1 = jnp.exp(s1 - st_ref[h, 1:2, cols])
            e2 = e2_ref[h, :, cols]
            w = w + jnp.where(e2 >= e2_min, e2, 0.0) * e1
        ars = slice(ii * keys, (ii + 1) * keys)
        w_scr[cols, ars] = (w * _gelu_exact(a_scr[ars, cols])).T.astype(BF)

    def up_rows(k):
        rs = slice(k * half, (k + 1) * half)
        w_rows = w_scr[rs, :]
        nc = min(512, o_ref.shape[1])
        for n in range(o_ref.shape[1] // nc):
            cs = slice(n * nc, (n + 1) * nc)
            o_ref[rs, cs] += jnp.dot(w_rows, v_ref[:, cs], preferred_element_type=F32)

    def down_cols(k):
        cols = slice(k * half, (k + 1) * half)
        a_scr[:, cols] = lax.dot_general(u_ref[...], h_ref[cols, :], (((1,), (1,)), ((), ())),
                                         preferred_element_type=F32)

    for k in range(n_halves):
        down_cols(k)
    for k in range(n_halves):
        for ii in range(et // keys):
            for c in range(k * half // LANES, (k + 1) * half // LANES):
                weight_block(ii, c)
        up_rows(k)


def _ln2_kernel(x_ref, ff_ref, g2_ref, lg_ref, lb_ref, o_ref, *, alpha):
    _ln_inplace(o_ref, x_ref, g2_ref, lg_ref, lb_ref, ff_ref, alpha)


def _mod_spec(D, which, row_fn):
    return pl.BlockSpec((1, 1, D), lambda i, *_: (row_fn(i) * 6 + which, 0, 0))


def _path(x, mod, row_of_seq, shared_mod, rope_tab, wts, alpha, cache=None):
    B, T, D = x.shape
    M = B * T
    x2 = x.reshape(M, D)
    heads = wts["heads"]
    qr, kvr = wts["qr"], wts["kvr"]
    CW = wts["cw"]
    MW = heads * V_DIM
    tm = min(TOKEN_TILE, M) if shared_mod else min(TOKEN_TILE, T)
    assert M % tm == 0 and (T % tm == 0 or tm % T == 0)
    assert rope_tab.shape[0] == max(T, tm)

    def row_fn(i):
        return row_of_seq((i * tm) // T)

    def rope_idx(i, *_):
        return (i % max(T // tm, 1), 0)

    mspec = functools.partial(_mod_spec, D, row_fn=row_fn)
    row_spec = pl.BlockSpec((tm, D), lambda i, *_: (i, 0))

    wa = wts["w_a"]
    NA = wa.shape[1]
    qcn, ckv, kpe, kpad, h1 = pl.pallas_call(
        functools.partial(_inproj_a_kernel, qr=qr, kvr=kvr),
        grid=(M // tm,),
        in_specs=[row_spec, mspec(1), mspec(0),
                  pl.BlockSpec((D, NA), lambda i: (0, 0), pipeline_mode=pl.Buffered(1)),
                  pl.BlockSpec((1, qr), lambda i: (0, 0)),
                  pl.BlockSpec((1, kvr), lambda i: (0, 0)),
                  pl.BlockSpec((tm, 2 * LANES), rope_idx)],
        out_specs=[pl.BlockSpec((tm, qr), lambda i: (i, 0)),
                   pl.BlockSpec((tm, kvr), lambda i: (i, 0)),
                   pl.BlockSpec((tm, ROPE_DIM), lambda i: (i, 0)),
                   pl.BlockSpec((tm, LANES), lambda i: (i, 0)),
                   row_spec],
        out_shape=[jax.ShapeDtypeStruct((M, qr), BF),
                   jax.ShapeDtypeStruct((M, kvr), F32),
                   jax.ShapeDtypeStruct((M, ROPE_DIM), F32),
                   jax.ShapeDtypeStruct((M, LANES), BF),
                   jax.ShapeDtypeStruct((M, D), BF)],
        compiler_params=_cp(("arbitrary",), 56),
        name="inproj_a",
    )(x2, mod, mod, wa, wts["g_q"], wts["g_kv"], rope_tab)

    wu = wts["w_u"]
    tn = min(512, CW)
    tmu = math.gcd(M, GLU_TOKEN_TILE)
    y = pl.pallas_call(
        _inproj_u_kernel,
        grid=(M // tmu, CW // tn),
        in_specs=[pl.BlockSpec((tmu, D), lambda i, j: (i, 0)),
                  pl.BlockSpec((D, tn), lambda i, j: (0, j)),
                  pl.BlockSpec((D, tn), lambda i, j: (0, j + CW // tn))],
        out_specs=pl.BlockSpec((tmu, tn), lambda i, j: (i, j)),
        out_shape=jax.ShapeDtypeStruct((M, CW), F32),
        compiler_params=_cp(("arbitrary", "arbitrary"), 56),
        name="inproj_u",
    )(h1, wu, wu)

    wq = wts["w_q"]
    q = pl.pallas_call(
        functools.partial(_qproj_kernel, heads=heads),
        grid=(M // tm,),
        in_specs=[pl.BlockSpec((tm, qr), lambda i: (i, 0)),
                  pl.BlockSpec(wq.shape, lambda i: (0, 0)),
                  pl.BlockSpec((tm, 2 * LANES), rope_idx)],
        out_specs=pl.BlockSpec((tm, heads * HEAD_Q), lambda i: (i, 0)),
        out_shape=jax.ShapeDtypeStruct((M, heads * HEAD_Q), BF),
        compiler_params=_cp(("arbitrary",), 48),
        name="qproj",
    )(qcn, wq, rope_tab)

    if cache is not None:
        cache_ckv, cache_kpe = cache
        P = cache_ckv.shape[1]
        ckv_all = jnp.concatenate([cache_ckv, ckv.reshape(B, T, kvr)], axis=1)
        kpad_all = jnp.concatenate(
            [jnp.pad(cache_kpe, ((0, 0), (0, 0), (0, LANES - ROPE_DIM))).astype(BF),
             kpad.reshape(B, T, LANES)], axis=1)
    else:
        P = 0
        ckv_all = ckv.reshape(B, T, kvr)
        kpad_all = kpad.reshape(B, T, LANES)
    S = P + T
    MS = B * S
    tk = math.gcd(MS, TOKEN_TILE)
    wkv = wts["w_kv"]
    kv = pl.pallas_call(
        functools.partial(_kvproj_kernel, heads=heads),
        grid=(MS // tk,),
        in_specs=[pl.BlockSpec((tk, kvr), lambda i: (i, 0)),
                  pl.BlockSpec(wkv.shape, lambda i: (0, 0))],
        out_specs=pl.BlockSpec((tk, heads * HEAD_KV), lambda i: (i, 0)),
        out_shape=jax.ShapeDtypeStruct((MS, heads * HEAD_KV), BF),
        compiler_params=_cp(("arbitrary",), 40),
        name="kvproj",
    )(ckv_all.reshape(MS, kvr), wkv)

    tq = min(ATTN_Q_TILE, T)
    attn = pl.pallas_call(
        functools.partial(_attn_kernel, heads=heads, scale=(NOPE_DIM + ROPE_DIM) ** -0.5),
        grid=(B, T // tq),
        in_specs=[pl.BlockSpec((1, tq, heads * HEAD_Q), lambda b, t: (b, t, 0)),
                  pl.BlockSpec((1, S, heads * HEAD_KV), lambda b, t: (b, 0, 0)),
                  pl.BlockSpec((1, S, LANES), lambda b, t: (b, 0, 0))],
        out_specs=pl.BlockSpec((1, tq, MW), lambda b, t: (b, t, 0)),
        out_shape=jax.ShapeDtypeStruct((B, T, MW), BF),
        compiler_params=_cp(("arbitrary", "arbitrary"), 48),
        name="attn",
    )(q.reshape(B, T, heads * HEAD_Q), kv.reshape(B, S, heads * HEAD_KV), kpad_all)

    tt = min(CONV_T_TILE, T)
    taps = wts["w_dw"].shape[0]
    conv = pl.pallas_call(
        functools.partial(_conv_kernel, tt=tt, taps=taps),
        grid=(B, T // tt),
        in_specs=[pl.BlockSpec((1, T, CW), lambda b, t: (b, 0, 0)),
                  pl.BlockSpec((taps, CW), lambda b, t: (0, 0)),
                  pl.BlockSpec((1, CW), lambda b, t: (0, 0)),
                  pl.BlockSpec((1, CW), lambda b, t: (0, 0)),
                  pl.BlockSpec((1, CW), lambda b, t: (0, 0))],
        out_specs=pl.BlockSpec((1, tt, CW), lambda b, t: (b, t, 0)),
        out_shape=jax.ShapeDtypeStruct((B, T, CW), BF),
        scratch_shapes=[pltpu.VMEM((tt + 2 * CONV_HALO, CW), F32), pltpu.VMEM((tt, CW), F32)],
        compiler_params=_cp(("arbitrary", "arbitrary"), 40),
        name="conv",
    )(y.reshape(B, T, CW), wts["w_dw"], wts["b_dw"], wts["g_cn"], wts["b_cn"])

    wo = wts["w_out"]
    tno = min(OUTPROJ_N_TILE, D)
    assert MW % CW == 0
    vec_spec = pl.BlockSpec((1, D), lambda i, *_: (0, 0))
    x1 = pl.pallas_call(
        functools.partial(_outproj_kernel, tn=tno, alpha=alpha),
        grid=(M // tm, D // tno),
        in_specs=[pl.BlockSpec((tm, MW), lambda i, j: (i, 0)),
                  pl.BlockSpec((tm, CW), lambda i, j: (i, 0)),
                  pl.BlockSpec((MW, tno), lambda i, j: (0, j)),
                  pl.BlockSpec((CW, tno), lambda i, j: (MW // CW, j)),
                  row_spec, mspec(2), vec_spec, vec_spec],
        out_specs=row_spec,
        out_shape=jax.ShapeDtypeStruct((M, D), F32),
        compiler_params=_cp(("arbitrary", "arbitrary"), 60),
        name="outproj",
    )(attn.reshape(M, MW), conv.reshape(M, CW), wo, wo, x2, mod, wts["ln1_g"], wts["ln1_b"])

    wpq = wts["w_pq"]
    keys_bf = wts["sub_keys"]
    nhp, nkeys, half = keys_bf.shape
    ph = nhp // 2
    gq = math.gcd(nhp, PEER_Q_GROUP)
    h2, sT = pl.pallas_call(
        _peer_q_kernel,
        grid=(M // tm, nhp // gq),
        in_specs=[row_spec, mspec(4), mspec(3),
                  pl.BlockSpec((D, gq * half), lambda i, j: (0, j)),
                  pl.BlockSpec((gq, nkeys, half), lambda i, j: (j, 0, 0))],
        out_specs=[pl.BlockSpec((tm, D), lambda i, j: (i, 0)),
                   pl.BlockSpec((gq, nkeys, tm), lambda i, j: (j, 0, i))],
        out_shape=[jax.ShapeDtypeStruct((M, D), BF),
                   jax.ShapeDtypeStruct((nhp, nkeys, M), F32)],
        compiler_params=_cp(("arbitrary", "arbitrary"), 48),
        name="peer_q",
    )(x1, mod, mod, wpq, keys_bf)

    tl = min(256, M)
    stats, e2 = pl.pallas_call(
        _route_kernel,
        grid=(M // tl, ph),
        in_specs=[pl.BlockSpec((2, nkeys, tl), lambda i, h: (h, 0, i))],
        out_specs=[pl.BlockSpec((1, SUBLANES, tl), lambda i, h: (h, 0, i)),
                   pl.BlockSpec((1, nkeys, tl), lambda i, h: (h, 0, i))],
        out_shape=[jax.ShapeDtypeStruct((ph, SUBLANES, M), F32),
                   jax.ShapeDtypeStruct((ph, nkeys, M), F32)],
        scratch_shapes=[pltpu.VMEM((24, tl), F32),
                        pltpu.VMEM((-(-_n_cand(PEER_TOPK + 1) // SUBLANES) * SUBLANES, tl), F32)],
        compiler_params=_cp(("arbitrary", "arbitrary"), 32),
        name="peer_route",
    )(sT)

    pu, pv = wts["peer_u"], wts["peer_v"]
    NE = pu.shape[0]
    nt = min(PEER_TOKEN_TILE, M)
    et = min(PEER_EXPERT_TILE, NE)
    rows = et // nkeys
    assert et % nkeys == 0 and NE == nkeys * nkeys and rows % SUBLANES == 0
    ff = pl.pallas_call(
        functools.partial(_peer_kernel, heads=ph, keys=nkeys),
        grid=(M // nt, NE // et),
        in_specs=[pl.BlockSpec((nt, D), lambda i, e: (i, 0)),
                  pl.BlockSpec((ph, None, rows, nt), lambda i, e: (0, 0, e, i)),
                  pl.BlockSpec((ph, nkeys, nt), lambda i, e: (0, 0, i)),
                  pl.BlockSpec((ph, SUBLANES, nt), lambda i, e: (0, 0, i)),
                  pl.BlockSpec((et, D), lambda i, e: (e, 0)),
                  pl.BlockSpec((et, D), lambda i, e: (e, 0))],
        out_specs=pl.BlockSpec((nt, D), lambda i, e: (i, 0), pipeline_mode=pl.Buffered(1)),
        out_shape=jax.ShapeDtypeStruct((M, D), F32),
        scratch_shapes=[pltpu.VMEM((et, nt), F32),
                        pltpu.VMEM((nt, et), BF)],
        compiler_params=_cp(("arbitrary", "arbitrary"), 60),
        name="peer_main",
    )(h2, sT.reshape(ph, 2, nkeys, M), e2, stats, pu, pv)

    t2 = min(256, T)

    def row_fn2(i):
        return row_of_seq((i * t2) // T)

    row2 = pl.BlockSpec((t2, D), lambda i: (i, 0))
    y_out = pl.pallas_call(
        functools.partial(_ln2_kernel, alpha=alpha),
        grid=(M // t2,),
        in_specs=[row2, row2, _mod_spec(D, 5, row_fn2),
                  pl.BlockSpec((1, D), lambda i: (0, 0)), pl.BlockSpec((1, D), lambda i: (0, 0))],
        out_specs=row2,
        out_shape=jax.ShapeDtypeStruct((M, D), F32),
        compiler_params=_cp(("arbitrary",), 40),
        name="ln2",
    )(x1, ff, mod, wts["ln2_g"], wts["ln2_b"])

    return y_out.reshape(B, T, D), ckv.reshape(B, T, kvr), kpe.reshape(B, T, ROPE_DIM)


def _rope_table(n_tokens):
    rows = n_tokens // GRID_W
    row = jnp.repeat(jnp.arange(rows, dtype=F32), GRID_W)
    col = jnp.tile(jnp.arange(GRID_W, dtype=F32), rows)
    n_freq = ROPE_DIM // 4
    inv = ROPE_BASE ** (-jnp.arange(n_freq, dtype=F32) / n_freq)
    ang_r = row[:, None] * inv
    ang_c = col[:, None] * inv
    ang = jnp.concatenate([ang_r, ang_r, ang_c, ang_c], -1)
    return _lane_pad_table(jnp.cos(ang), jnp.sin(ang))


def _lane_pad_table(cos, sin):
    zero = jnp.zeros((cos.shape[0], LANES - ROPE_DIM), F32)
    return jnp.concatenate([cos, zero, sin, zero], -1)


def _prep_weights(l, w_in, g_q, w_uq, g_kv, w_ukv, w_dw, b_dw, g_cn, b_cn, w_out,
                  ln1_g, ln1_b, w_pq, sub_keys, peer_u, peer_v, ln2_g, ln2_b):
    qr = g_q.shape[-1]
    kvr = g_kv.shape[-1]
    o3 = qr + kvr + ROPE_DIM
    heads = w_uq.shape[-1] // (NOPE_DIM + ROPE_DIM)
    wi = w_in[l]
    n_a = qr + kvr + LANES
    assert (qr + kvr) % LANES == 0 and n_a <= wi.shape[1]
    wi_bf = wi.astype(BF)
    w_a = wi_bf[:, :n_a]
    w_u = wi_bf[:, o3:]
    wq3 = w_uq[l].reshape(qr, heads, NOPE_DIM + ROPE_DIM).astype(BF)
    w_q = jnp.pad(wq3, ((0, 0), (0, 0), (0, HEAD_Q - NOPE_DIM - ROPE_DIM)))
    w_q = w_q.reshape(qr, heads * HEAD_Q)
    ph, two, nkeys, half = sub_keys.shape[1:]
    return dict(
        heads=heads, qr=qr, kvr=kvr, cw=w_dw.shape[-1],
        w_a=w_a, w_u=w_u, w_q=w_q, w_kv=w_ukv[l].astype(BF),
        g_q=g_q[l][None], g_kv=g_kv[l][None],
        w_dw=w_dw[l], b_dw=b_dw[l][None], g_cn=g_cn[l][None], b_cn=b_cn[l][None],
        w_out=w_out[l].astype(BF), ln1_g=ln1_g[l][None], ln1_b=ln1_b[l][None],
        w_pq=w_pq[l].astype(BF), sub_keys=sub_keys[l].reshape(ph * two, nkeys, half).astype(BF),
        peer_u=peer_u[l].astype(BF), peer_v=peer_v[l].astype(BF),
        ln2_g=ln2_g[l][None], ln2_b=ln2_b[l][None],
    )


def kernel(x_prompt, x_sample, cache_ckv, cache_kpe, c, c_ctx, w_ada, b_ada, w_in, g_q, w_uq,
           g_kv, w_ukv, w_dw, b_dw, g_cn, b_cn, w_out, ln1_g, ln1_b, w_pq, sub_keys, peer_u,
           peer_v, ln2_g, ln2_b):
    depth = w_ada.shape[0]
    alpha = (2 * depth) ** 0.25
    B, T, D = x_prompt.shape
    Bd, Td, _ = x_sample.shape
    n_rows = -(-(1 + Bd) // SUBLANES) * SUBLANES
    cond = jnp.concatenate([c_ctx[None, :], c, jnp.zeros((n_rows - 1 - Bd, D), F32)], axis=0)

    n_tab = max(T, min(TOKEN_TILE, B * T))
    ones_tab = _lane_pad_table(jnp.ones((n_tab, ROPE_DIM), F32), jnp.zeros((n_tab, ROPE_DIM), F32))
    rope_tab = _rope_table(Td)

    xp, xs = x_prompt, x_sample
    ckv_layers, kpe_layers = [], []
    for l in range(depth):
        wts = _prep_weights(l, w_in, g_q, w_uq, g_kv, w_ukv, w_dw, b_dw, g_cn, b_cn, w_out,
                            ln1_g, ln1_b, w_pq, sub_keys, peer_u, peer_v, ln2_g, ln2_b)
        mod = _adaln(cond, w_ada[l], b_ada[l]).reshape(n_rows * 6, 1, D)
        xp, ckv, kpe = _path(xp, mod, lambda b: 0, True, ones_tab, wts, alpha)
        ckv_layers.append(ckv)
        kpe_layers.append(kpe)
        xs, _, _ = _path(xs, mod, lambda b: 1 + b, False, rope_tab, wts, alpha,
                         cache=(cache_ckv[:, l], cache_kpe[:, l]))
    return (xp, xs, jnp.stack(ckv_layers, axis=1), jnp.stack(kpe_layers, axis=1))
```

```python
import functools
import math

import jax
import jax.numpy as jnp
from jax import lax
from jax.experimental import pallas as pl
from jax.experimental.pallas import tpu as pltpu

F32 = jnp.float32
BF = jnp.bfloat16

NOPE_DIM = 128
ROPE_DIM = 64
V_DIM = 128
HEAD_Q = NOPE_DIM + 2 * ROPE_DIM
HEAD_KV = NOPE_DIM + V_DIM
GRID_W = 64
ROPE_BASE = 10000.0
PEER_TOPK = 16
EPS = 1e-6
LOG2_E = 1.4426950408889634

LANES = 128
SUBLANES = 8
VMEM_MIB = 1 << 20

TOKEN_TILE = 512
GLU_TOKEN_TILE = 1024
OUTPROJ_N_TILE = 512
PEER_TOKEN_TILE = 512
PEER_EXPERT_TILE = 1024
PEER_Q_GROUP = 4
ROUTE_HEAD_GROUP = 4
DOT_HEAD_GROUP = 4
ATTN_Q_TILE = 256
CONV_T_TILE = 256
CONV_T_CHUNK = 64
CONV_C_CHUNK = 128
CONV_HALO = 16


def _cp(sem, vmem_mib):
    return pltpu.CompilerParams(dimension_semantics=sem, vmem_limit_bytes=vmem_mib * VMEM_MIB)


def _rope_lanes(x, tab):
    q = ROPE_DIM // 4
    lane = lax.broadcasted_iota(jnp.int32, x.shape, 1)
    x = jnp.where(lane < ROPE_DIM, x, 0.0)
    nxt = pltpu.roll(x, LANES - q, 1)
    prv = pltpu.roll(x, q, 1)
    rot = jnp.where((lane // q) % 2 == 0, -nxt, prv)
    return x * tab[:, :LANES] + rot * tab[:, LANES:]


def _layer_norm_rows(z, g, b):
    mu = jnp.mean(z, -1, keepdims=True)
    zc = z - mu
    var = jnp.mean(zc * zc, -1, keepdims=True)
    return zc * lax.rsqrt(var + EPS) * g + b


def _rms_norm_rows(z, g):
    return z * lax.rsqrt(jnp.mean(z * z, -1, keepdims=True) + EPS) * g


def _adaln_kernel(c_ref, w_ref, b_ref, o_ref):
    c = c_ref[...]
    s = (c * jax.nn.sigmoid(c)).astype(BF)
    o_ref[...] = jnp.dot(s, w_ref[...].astype(BF), preferred_element_type=F32) + b_ref[...]


def _adaln(cond, w, b):
    R, D = cond.shape
    N = w.shape[1]
    tn = min(512, N)
    return pl.pallas_call(
        _adaln_kernel,
        grid=(N // tn,),
        in_specs=[pl.BlockSpec((R, D), lambda j: (0, 0)),
                  pl.BlockSpec((D, tn), lambda j: (0, j)),
                  pl.BlockSpec((1, tn), lambda j: (0, j))],
        out_specs=pl.BlockSpec((R, tn), lambda j: (0, j)),
        out_shape=jax.ShapeDtypeStruct((R, N), F32),
        compiler_params=_cp(("arbitrary",), 40),
        name="adaln",
    )(cond, w, b.reshape(1, N))


def _inproj_a_kernel(x_ref, sc_ref, sh_ref, w_ref, gq_ref, gkv_ref, rope_ref,
                     qcn_ref, ckv_ref, kpe_ref, kpad_ref, h_ref, *, qr, kvr):
    h_ref[...] = (x_ref[...] * (1.0 + sc_ref[0]) + sh_ref[0]).astype(BF)
    p = jnp.dot(h_ref[...], w_ref[...], preferred_element_type=F32)
    qcn_ref[...] = _rms_norm_rows(p[:, :qr], gq_ref[...]).astype(BF)
    ckv_ref[...] = _rms_norm_rows(p[:, qr:qr + kvr], gkv_ref[...])
    kslab = p[:, qr + kvr:qr + kvr + LANES]
    kpe_ref[...] = kslab[:, :ROPE_DIM]
    kpad_ref[...] = _rope_lanes(kslab, rope_ref[...]).astype(BF)


def _inproj_u_kernel(h_ref, wa_ref, wg_ref, y_ref):
    h = h_ref[...]
    a = jnp.dot(h, wa_ref[...], preferred_element_type=F32)
    g = jnp.dot(h, wg_ref[...], preferred_element_type=F32)
    y_ref[...] = a * jax.nn.sigmoid(g)


def _qproj_kernel(x_ref, w_ref, rope_ref, o_ref, *, heads):
    x = x_ref[...]
    rope = rope_ref[...]
    group = math.gcd(heads, DOT_HEAD_GROUP)
    for g in range(heads // group):
        qg = jnp.dot(x, w_ref[:, g * group * HEAD_Q:(g + 1) * group * HEAD_Q],
                     preferred_element_type=F32)
        for j in range(group):
            b = (g * group + j) * HEAD_Q
            q = qg[:, j * HEAD_Q:(j + 1) * HEAD_Q]
            o_ref[:, b:b + NOPE_DIM] = q[:, :NOPE_DIM].astype(BF)
            o_ref[:, b + NOPE_DIM:b + HEAD_Q] = _rope_lanes(q[:, NOPE_DIM:], rope).astype(BF)


def _kvproj_kernel(x_ref, w_ref, o_ref, *, heads):
    x = x_ref[...].astype(BF)
    width = math.gcd(heads, DOT_HEAD_GROUP) * HEAD_KV
    for g in range(heads * HEAD_KV // width):
        o_ref[:, g * width:(g + 1) * width] = jnp.dot(
            x, w_ref[:, g * width:(g + 1) * width], preferred_element_type=F32).astype(BF)


def _attn_kernel(q_ref, kv_ref, kpad_ref, o_ref, *, heads, scale):
    kpad = kpad_ref[0]
    for h in range(heads):
        qh = q_ref[0, :, h * HEAD_Q:(h + 1) * HEAD_Q]
        kh = jnp.concatenate([kv_ref[0, :, h * HEAD_KV:h * HEAD_KV + NOPE_DIM], kpad], axis=1)
        s = lax.dot_general(qh, kh, (((1,), (1,)), ((), ())), preferred_element_type=F32)
        m = jnp.max(s, -1, keepdims=True)
        p = jnp.exp2((s - m) * (scale * LOG2_E))
        l = jnp.sum(p, -1, keepdims=True)
        o = jnp.dot(p.astype(BF), kv_ref[0, :, h * HEAD_KV + NOPE_DIM:(h + 1) * HEAD_KV],
                    preferred_element_type=F32)
        o_ref[0, :, h * V_DIM:(h + 1) * V_DIM] = (o / l).astype(BF)


def _conv_kernel(y_ref, w_ref, bdw_ref, g_ref, b_ref, o_ref, pad_scr, z_scr, *, tt, taps):
    t = pl.program_id(1)
    nt = pl.num_programs(1)
    C = z_scr.shape[1]
    half = taps // 2
    pad_scr[CONV_HALO:CONV_HALO + tt, :] = y_ref[0, pl.ds(pl.multiple_of(t * tt, tt), tt), :]

    @pl.when(t == 0)
    def _():
        pad_scr[0:CONV_HALO, :] = jnp.zeros((CONV_HALO, C), F32)

    @pl.when(t > 0)
    def _():
        pad_scr[0:CONV_HALO, :] = y_ref[
            0, pl.ds(pl.multiple_of(t * tt - CONV_HALO, SUBLANES), CONV_HALO), :]

    @pl.when(t == nt - 1)
    def _():
        pad_scr[CONV_HALO + tt:, :] = jnp.zeros((CONV_HALO, C), F32)

    @pl.when(t < nt - 1)
    def _():
        pad_scr[CONV_HALO + tt:, :] = y_ref[
            0, pl.ds(pl.multiple_of((t + 1) * tt, tt), CONV_HALO), :]

    tc = CONV_T_CHUNK
    cc = min(CONV_C_CHUNK, C)
    base = CONV_HALO - half
    wlen = tc + 2 * CONV_HALO
    assert base >= 0 and base + taps - 1 + tc <= wlen

    def chan_body(ci, carry):
        c0 = pl.multiple_of(ci * cc, cc)
        for r in range(tt // tc):
            win = pad_scr[r * tc:r * tc + wlen, pl.ds(c0, cc)]
            acc = jnp.zeros((tc, cc), F32)
            for sh in range(SUBLANES):
                rolled = win if sh == 0 else pltpu.roll(win, wlen - sh, 0)
                for a in range(wlen // SUBLANES):
                    k = a * SUBLANES + sh - base
                    if 0 <= k < taps:
                        acc = acc + (rolled[a * SUBLANES:a * SUBLANES + tc, :]
                                     * w_ref[k:k + 1, pl.ds(c0, cc)])
            z_scr[r * tc:(r + 1) * tc, pl.ds(c0, cc)] = acc + bdw_ref[:, pl.ds(c0, cc)]
        return carry

    lax.fori_loop(0, C // cc, chan_body, 0)

    rows = 64
    for r in range(tt // rows):
        zn = _layer_norm_rows(z_scr[r * rows:(r + 1) * rows, :], g_ref[...], b_ref[...])
        o_ref[0, r * rows:(r + 1) * rows, :] = (zn * jax.nn.sigmoid(zn)).astype(BF)


def _ln_inplace(xo_ref, x_ref, gate_ref, lg_ref, lb_ref, ff_ref, alpha):
    rows = 64
    for r in range(xo_ref.shape[0] // rows):
        sl = slice(r * rows, (r + 1) * rows)
        z = alpha * x_ref[sl, :] + gate_ref[0] * ff_ref[sl, :]
        xo_ref[sl, :] = _layer_norm_rows(z, lg_ref[...], lb_ref[...])


def _outproj_kernel(a_ref, c_ref, wa_ref, wc_ref, x_ref, g1_ref, lg_ref, lb_ref, xo_ref,
                    *, tn, alpha):
    j = pl.program_id(1)
    col = pl.multiple_of(j * tn, tn)
    xo_ref[:, pl.ds(col, tn)] = (
        jnp.dot(a_ref[...], wa_ref[...], preferred_element_type=F32)
        + jnp.dot(c_ref[...], wc_ref[...], preferred_element_type=F32))

    @pl.when(j == pl.num_programs(1) - 1)
    def _():
        _ln_inplace(xo_ref, x_ref, g1_ref, lg_ref, lb_ref, xo_ref, alpha)


def _peer_q_kernel(x_ref, sc_ref, sh_ref, w_ref, k_ref, h_ref, s_ref):
    @pl.when(pl.program_id(1) == 0)
    def _():
        h_ref[...] = (x_ref[...] * (1.0 + sc_ref[0]) + sh_ref[0]).astype(BF)

    half = k_ref.shape[2]
    q = jnp.dot(h_ref[...], w_ref[...], preferred_element_type=F32).astype(BF)
    for p in range(k_ref.shape[0]):
        s_ref[p] = lax.dot_general(k_ref[p], q[:, p * half:(p + 1) * half],
                                   (((1,), (1,)), ((), ())), preferred_element_type=F32)


def _route_kernel(s_ref, st_ref, e2_ref, t2_scr, cand_scr):
    for hh in range(st_ref.shape[0]):
        _route_one_head(s_ref, st_ref, e2_ref, t2_scr, cand_scr, hh)


def _route_one_head(s_ref, st_ref, e2_ref, t2_scr, cand_scr, hh):
    n_top = PEER_TOPK + 1
    neg = -jnp.inf
    tl = s_ref.shape[2]

    def top(p):
        n_grp = s_ref.shape[1] // SUBLANES
        col = [s_ref[2 * hh + p, g * SUBLANES:(g + 1) * SUBLANES, :] for g in range(n_grp)]
        for i, j in _sorting_network(n_grp):
            col[i], col[j] = jnp.maximum(col[i], col[j]), jnp.minimum(col[i], col[j])
        col.append(jnp.full(col[0].shape, neg, F32))
        out = []
        for it in range(n_top):
            m = jnp.max(col[0], axis=0, keepdims=True)
            out.append(m)
            took = col[0] == m
            for g in range(min(n_grp, n_top - 1 - it)):
                col[g] = jnp.where(took, col[g + 1], col[g])
        return out

    t1 = top(0)
    t2 = top(1)
    for k in range(n_top):
        t2_scr[k:k + 1, :] = t2[k]
    cand_scr[...] = jnp.full(cand_scr.shape, neg, F32)
    off = 0
    for a in range(n_top):
        nb = n_top // (a + 1)
        cand_scr[off:off + nb, :] = t2_scr[0:nb, :] + t1[a]
        off += nb
    cand = cand_scr[...]
    c = cand
    kth = None
    nxt = None
    for it in range(n_top):
        m = jnp.max(c, axis=0, keepdims=True)
        if it == PEER_TOPK - 1:
            kth = m
        if it == PEER_TOPK:
            nxt = m
        c = jnp.where(c == m, neg, c)
    tau = 0.5 * (kth + nxt)
    top_sum = t1[0] + t2[0]
    z = jnp.sum(jnp.where(cand > tau, jnp.exp(cand - top_sum), 0.0), axis=0, keepdims=True)
    st_ref[hh, 0:1, :] = tau
    st_ref[hh, 1:2, :] = t1[0]
    log_norm = t2[0] + jnp.log(z)
    st_ref[hh, 2:3, :] = log_norm
    st_ref[hh, 3:, :] = jnp.zeros((SUBLANES - 3, tl), F32)
    e2_ref[hh] = jnp.exp(s_ref[2 * hh + 1] - log_norm)


def _sorting_network(n):
    assert n & (n - 1) == 0

    def merge(lo, hi, r):
        step = r * 2
        if step < hi - lo:
            yield from merge(lo, hi, step)
            yield from merge(lo + r, hi, step)
            yield from ((i, i + r) for i in range(lo + r, hi - r, step))
        else:
            yield (lo, lo + r)

    def sort(lo, hi):
        if hi - lo >= 1:
            mid = lo + (hi - lo) // 2
            yield from sort(lo, mid)
            yield from sort(mid + 1, hi)
            yield from merge(lo, hi, 1)

    return list(sort(0, n - 1))


def _n_cand(n_top):
    return sum(n_top // (a + 1) for a in range(n_top))


def _gelu_exact(x):
    return 0.5 * x * (1.0 + lax.erf(x * (1.0 / math.sqrt(2.0))))


def _peer_kernel(h_ref, s1_ref, e2_ref, st_ref, u_ref, v_ref, o_ref, a_scr, w_scr, *, heads, keys):
    nt = h_ref.shape[0]
    et = u_ref.shape[0]
    n_halves = 2 if nt >= 2 * LANES else 1
    half = nt // n_halves

    @pl.when(pl.program_id(1) == 0)
    def _():
        o_ref[...] = jnp.zeros(o_ref.shape, F32)

    def weight_block(ii, c):
        cols = slice(c * LANES, (c + 1) * LANES)
        w = jnp.zeros((keys, LANES), F32)
        for h in range(heads):
            s1 = s1_ref[h, ii:ii + 1, cols]
            e2_min = jnp.exp(st_ref[h, 0:1, cols] - s1 - st_ref[h, 2:3, cols])
            e1 = jnp.exp(s1 - st_ref[h, 1:2, cols])
            e2 = e2_ref[h, :, cols]
            w = w + jnp.where(e2 >= e2_min, e2, 0.0) * e1
        ars = slice(ii * keys, (ii + 1) * keys)
        w_scr[cols, ars] = (w * _gelu_exact(a_scr[ars, cols])).T.astype(BF)

    def up_rows(k):
        rs = slice(k * half, (k + 1) * half)
        w_rows = w_scr[rs, :]
        nc = min(512, o_ref.shape[1])
        for n in range(o_ref.shape[1] // nc):
            cs = slice(n * nc, (n + 1) * nc)
            o_ref[rs, cs] += jnp.dot(w_rows, v_ref[:, cs], preferred_element_type=F32)

    def down_cols(k):
        cols = slice(k * half, (k + 1) * half)
        a_scr[:, cols] = lax.dot_general(u_ref[...], h_ref[cols, :], (((1,), (1,)), ((), ())),
                                         preferred_element_type=F32)

    for k in range(n_halves):
        down_cols(k)
    for k in range(n_halves):
        for ii in range(et // keys):
            for c in range(k * half // LANES, (k + 1) * half // LANES):
                weight_block(ii, c)
        up_rows(k)


def _ln2_kernel(x_ref, ff_ref, g2_ref, lg_ref, lb_ref, o_ref, *, alpha):
    _ln_inplace(o_ref, x_ref, g2_ref, lg_ref, lb_ref, ff_ref, alpha)


def _mod_spec(D, which, row_fn):
    return pl.BlockSpec((1, 1, D), lambda i, *_: (row_fn(i) * 6 + which, 0, 0))


def _path(x, mod, row_of_seq, shared_mod, rope_tab, wts, alpha, cache=None):
    B, T, D = x.shape
    M = B * T
    x2 = x.reshape(M, D)
    heads = wts["heads"]
    qr, kvr = wts["qr"], wts["kvr"]
    CW = wts["cw"]
    MW = heads * V_DIM
    tm = min(TOKEN_TILE, M) if shared_mod else min(TOKEN_TILE, T)
    assert M % tm == 0 and (T % tm == 0 or tm % T == 0)
    assert rope_tab.shape[0] == max(T, tm)

    def row_fn(i):
        return row_of_seq((i * tm) // T)

    def rope_idx(i, *_):
        return (i % max(T // tm, 1), 0)

    mspec = functools.partial(_mod_spec, D, row_fn=row_fn)
    row_spec = pl.BlockSpec((tm, D), lambda i, *_: (i, 0))

    wa = wts["w_a"]
    NA = wa.shape[1]
    qcn, ckv, kpe, kpad, h1 = pl.pallas_call(
        functools.partial(_inproj_a_kernel, qr=qr, kvr=kvr),
        grid=(M // tm,),
        in_specs=[row_spec, mspec(1), mspec(0),
                  pl.BlockSpec((D, NA), lambda i: (0, 0), pipeline_mode=pl.Buffered(1)),
                  pl.BlockSpec((1, qr), lambda i: (0, 0)),
                  pl.BlockSpec((1, kvr), lambda i: (0, 0)),
                  pl.BlockSpec((tm, 2 * LANES), rope_idx)],
        out_specs=[pl.BlockSpec((tm, qr), lambda i: (i, 0)),
                   pl.BlockSpec((tm, kvr), lambda i: (i, 0)),
                   pl.BlockSpec((tm, ROPE_DIM), lambda i: (i, 0)),
                   pl.BlockSpec((tm, LANES), lambda i: (i, 0)),
                   row_spec],
        out_shape=[jax.ShapeDtypeStruct((M, qr), BF),
                   jax.ShapeDtypeStruct((M, kvr), F32),
                   jax.ShapeDtypeStruct((M, ROPE_DIM), F32),
                   jax.ShapeDtypeStruct((M, LANES), BF),
                   jax.ShapeDtypeStruct((M, D), BF)],
        compiler_params=_cp(("arbitrary",), 56),
        name="inproj_a",
    )(x2, mod, mod, wa, wts["g_q"], wts["g_kv"], rope_tab)

    wu = wts["w_u"]
    tn = min(512, CW)
    tmu = math.gcd(M, GLU_TOKEN_TILE)
    y = pl.pallas_call(
        _inproj_u_kernel,
        grid=(M // tmu, CW // tn),
        in_specs=[pl.BlockSpec((tmu, D), lambda i, j: (i, 0)),
                  pl.BlockSpec((D, tn), lambda i, j: (0, j)),
                  pl.BlockSpec((D, tn), lambda i, j: (0, j + CW // tn))],
        out_specs=pl.BlockSpec((tmu, tn), lambda i, j: (i, j)),
        out_shape=jax.ShapeDtypeStruct((M, CW), F32),
        compiler_params=_cp(("arbitrary", "arbitrary"), 56),
        name="inproj_u",
    )(h1, wu, wu)

    wq = wts["w_q"]
    q = pl.pallas_call(
        functools.partial(_qproj_kernel, heads=heads),
        grid=(M // tm,),
        in_specs=[pl.BlockSpec((tm, qr), lambda i: (i, 0)),
                  pl.BlockSpec(wq.shape, lambda i: (0, 0)),
                  pl.BlockSpec((tm, 2 * LANES), rope_idx)],
        out_specs=pl.BlockSpec((tm, heads * HEAD_Q), lambda i: (i, 0)),
        out_shape=jax.ShapeDtypeStruct((M, heads * HEAD_Q), BF),
        compiler_params=_cp(("arbitrary",), 48),
        name="qproj",
    )(qcn, wq, rope_tab)

    if cache is not None:
        cache_ckv, cache_kpe = cache
        P = cache_ckv.shape[1]
        ckv_all = jnp.concatenate([cache_ckv, ckv.reshape(B, T, kvr)], axis=1)
        kpad_all = jnp.concatenate(
            [jnp.pad(cache_kpe, ((0, 0), (0, 0), (0, LANES - ROPE_DIM))).astype(BF),
             kpad.reshape(B, T, LANES)], axis=1)
    else:
        P = 0
        ckv_all = ckv.reshape(B, T, kvr)
        kpad_all = kpad.reshape(B, T, LANES)
    S = P + T
    MS = B * S
    tk = math.gcd(MS, TOKEN_TILE)
    wkv = wts["w_kv"]
    kv = pl.pallas_call(
        functools.partial(_kvproj_kernel, heads=heads),
        grid=(MS // tk,),
        in_specs=[pl.BlockSpec((tk, kvr), lambda i: (i, 0)),
                  pl.BlockSpec(wkv.shape, lambda i: (0, 0))],
        out_specs=pl.BlockSpec((tk, heads * HEAD_KV), lambda i: (i, 0)),
        out_shape=jax.ShapeDtypeStruct((MS, heads * HEAD_KV), BF),
        compiler_params=_cp(("arbitrary",), 40),
        name="kvproj",
    )(ckv_all.reshape(MS, kvr), wkv)

    tq = min(ATTN_Q_TILE, T)
    attn = pl.pallas_call(
        functools.partial(_attn_kernel, heads=heads, scale=(NOPE_DIM + ROPE_DIM) ** -0.5),
        grid=(B, T // tq),
        in_specs=[pl.BlockSpec((1, tq, heads * HEAD_Q), lambda b, t: (b, t, 0)),
                  pl.BlockSpec((1, S, heads * HEAD_KV), lambda b, t: (b, 0, 0)),
                  pl.BlockSpec((1, S, LANES), lambda b, t: (b, 0, 0))],
        out_specs=pl.BlockSpec((1, tq, MW), lambda b, t: (b, t, 0)),
        out_shape=jax.ShapeDtypeStruct((B, T, MW), BF),
        compiler_params=_cp(("arbitrary", "arbitrary"), 48),
        name="attn",
    )(q.reshape(B, T, heads * HEAD_Q), kv.reshape(B, S, heads * HEAD_KV), kpad_all)

    tt = min(CONV_T_TILE, T)
    taps = wts["w_dw"].shape[0]
    conv = pl.pallas_call(
        functools.partial(_conv_kernel, tt=tt, taps=taps),
        grid=(B, T // tt),
        in_specs=[pl.BlockSpec((1, T, CW), lambda b, t: (b, 0, 0)),
                  pl.BlockSpec((taps, CW), lambda b, t: (0, 0)),
                  pl.BlockSpec((1, CW), lambda b, t: (0, 0)),
                  pl.BlockSpec((1, CW), lambda b, t: (0, 0)),
                  pl.BlockSpec((1, CW), lambda b, t: (0, 0))],
        out_specs=pl.BlockSpec((1, tt, CW), lambda b, t: (b, t, 0)),
        out_shape=jax.ShapeDtypeStruct((B, T, CW), BF),
        scratch_shapes=[pltpu.VMEM((tt + 2 * CONV_HALO, CW), F32), pltpu.VMEM((tt, CW), F32)],
        compiler_params=_cp(("arbitrary", "arbitrary"), 40),
        name="conv",
    )(y.reshape(B, T, CW), wts["w_dw"], wts["b_dw"], wts["g_cn"], wts["b_cn"])

    wo = wts["w_out"]
    tno = min(OUTPROJ_N_TILE, D)
    assert MW % CW == 0
    vec_spec = pl.BlockSpec((1, D), lambda i, *_: (0, 0))
    x1 = pl.pallas_call(
        functools.partial(_outproj_kernel, tn=tno, alpha=alpha),
        grid=(M // tm, D // tno),
        in_specs=[pl.BlockSpec((tm, MW), lambda i, j: (i, 0)),
                  pl.BlockSpec((tm, CW), lambda i, j: (i, 0)),
                  pl.BlockSpec((MW, tno), lambda i, j: (0, j)),
                  pl.BlockSpec((CW, tno), lambda i, j: (MW // CW, j)),
                  row_spec, mspec(2), vec_spec, vec_spec],
        out_specs=row_spec,
        out_shape=jax.ShapeDtypeStruct((M, D), F32),
        compiler_params=_cp(("arbitrary", "arbitrary"), 60),
        name="outproj",
    )(attn.reshape(M, MW), conv.reshape(M, CW), wo, wo, x2, mod, wts["ln1_g"], wts["ln1_b"])

    wpq = wts["w_pq"]
    keys_bf = wts["sub_keys"]
    nhp, nkeys, half = keys_bf.shape
    ph = nhp // 2
    gq = math.gcd(nhp, PEER_Q_GROUP)
    h2, sT = pl.pallas_call(
        _peer_q_kernel,
        grid=(M // tm, nhp // gq),
        in_specs=[row_spec, mspec(4), mspec(3),
                  pl.BlockSpec((D, gq * half), lambda i, j: (0, j)),
                  pl.BlockSpec((gq, nkeys, half), lambda i, j: (j, 0, 0))],
        out_specs=[pl.BlockSpec((tm, D), lambda i, j: (i, 0)),
                   pl.BlockSpec((gq, nkeys, tm), lambda i, j: (j, 0, i))],
        out_shape=[jax.ShapeDtypeStruct((M, D), BF),
                   jax.ShapeDtypeStruct((nhp, nkeys, M), F32)],
        compiler_params=_cp(("arbitrary", "arbitrary"), 48),
        name="peer_q",
    )(x1, mod, mod, wpq, keys_bf)

    tl = min(256, M)
    gh = math.gcd(ph, ROUTE_HEAD_GROUP)
    stats, e2 = pl.pallas_call(
        _route_kernel,
        grid=(M // tl, ph // gh),
        in_specs=[pl.BlockSpec((2 * gh, nkeys, tl), lambda i, h: (h, 0, i))],
        out_specs=[pl.BlockSpec((gh, SUBLANES, tl), lambda i, h: (h, 0, i)),
                   pl.BlockSpec((gh, nkeys, tl), lambda i, h: (h, 0, i))],
        out_shape=[jax.ShapeDtypeStruct((ph, SUBLANES, M), F32),
                   jax.ShapeDtypeStruct((ph, nkeys, M), F32)],
        scratch_shapes=[pltpu.VMEM((24, tl), F32),
                        pltpu.VMEM((-(-_n_cand(PEER_TOPK + 1) // SUBLANES) * SUBLANES, tl), F32)],
        compiler_params=_cp(("arbitrary", "arbitrary"), 32),
        name="peer_route",
    )(sT)

    pu, pv = wts["peer_u"], wts["peer_v"]
    NE = pu.shape[0]
    nt = min(PEER_TOKEN_TILE, M)
    et = min(PEER_EXPERT_TILE, NE)
    rows = et // nkeys
    assert et % nkeys == 0 and NE == nkeys * nkeys and rows % SUBLANES == 0
    ff = pl.pallas_call(
        functools.partial(_peer_kernel, heads=ph, keys=nkeys),
        grid=(M // nt, NE // et),
        in_specs=[pl.BlockSpec((nt, D), lambda i, e: (i, 0)),
                  pl.BlockSpec((ph, None, rows, nt), lambda i, e: (0, 0, e, i)),
                  pl.BlockSpec((ph, nkeys, nt), lambda i, e: (0, 0, i)),
                  pl.BlockSpec((ph, SUBLANES, nt), lambda i, e: (0, 0, i)),
                  pl.BlockSpec((et, D), lambda i, e: (e, 0)),
                  pl.BlockSpec((et, D), lambda i, e: (e, 0))],
        out_specs=pl.BlockSpec((nt, D), lambda i, e: (i, 0), pipeline_mode=pl.Buffered(1)),
        out_shape=jax.ShapeDtypeStruct((M, D), F32),
        scratch_shapes=[pltpu.VMEM((et, nt), F32),
                        pltpu.VMEM((nt, et), BF)],
        compiler_params=_cp(("arbitrary", "arbitrary"), 60),
        name="peer_main",
    )(h2, sT.reshape(ph, 2, nkeys, M), e2, stats, pu, pv)

    t2 = min(256, T)

    def row_fn2(i):
        return row_of_seq((i * t2) // T)

    row2 = pl.BlockSpec((t2, D), lambda i: (i, 0))
    y_out = pl.pallas_call(
        functools.partial(_ln2_kernel, alpha=alpha),
        grid=(M // t2,),
        in_specs=[row2, row2, _mod_spec(D, 5, row_fn2),
                  pl.BlockSpec((1, D), lambda i: (0, 0)), pl.BlockSpec((1, D), lambda i: (0, 0))],
        out_specs=row2,
        out_shape=jax.ShapeDtypeStruct((M, D), F32),
        compiler_params=_cp(("arbitrary",), 40),
        name="ln2",
    )(x1, ff, mod, wts["ln2_g"], wts["ln2_b"])

    return y_out.reshape(B, T, D), ckv.reshape(B, T, kvr), kpe.reshape(B, T, ROPE_DIM)


def _rope_table(n_tokens):
    rows = n_tokens // GRID_W
    row = jnp.repeat(jnp.arange(rows, dtype=F32), GRID_W)
    col = jnp.tile(jnp.arange(GRID_W, dtype=F32), rows)
    n_freq = ROPE_DIM // 4
    inv = ROPE_BASE ** (-jnp.arange(n_freq, dtype=F32) / n_freq)
    ang_r = row[:, None] * inv
    ang_c = col[:, None] * inv
    ang = jnp.concatenate([ang_r, ang_r, ang_c, ang_c], -1)
    return _lane_pad_table(jnp.cos(ang), jnp.sin(ang))


def _lane_pad_table(cos, sin):
    zero = jnp.zeros((cos.shape[0], LANES - ROPE_DIM), F32)
    return jnp.concatenate([cos, zero, sin, zero], -1)


def _prep_weights(l, w_in, g_q, w_uq, g_kv, w_ukv, w_dw, b_dw, g_cn, b_cn, w_out,
                  ln1_g, ln1_b, w_pq, sub_keys, peer_u, peer_v, ln2_g, ln2_b):
    qr = g_q.shape[-1]
    kvr = g_kv.shape[-1]
    o3 = qr + kvr + ROPE_DIM
    heads = w_uq.shape[-1] // (NOPE_DIM + ROPE_DIM)
    wi = w_in[l]
    n_a = qr + kvr + LANES
    assert (qr + kvr) % LANES == 0 and n_a <= wi.shape[1]
    wi_bf = wi.astype(BF)
    w_a = wi_bf[:, :n_a]
    w_u = wi_bf[:, o3:]
    wq3 = w_uq[l].reshape(qr, heads, NOPE_DIM + ROPE_DIM).astype(BF)
    w_q = jnp.pad(wq3, ((0, 0), (0, 0), (0, HEAD_Q - NOPE_DIM - ROPE_DIM)))
    w_q = w_q.reshape(qr, heads * HEAD_Q)
    ph, two, nkeys, half = sub_keys.shape[1:]
    return dict(
        heads=heads, qr=qr, kvr=kvr, cw=w_dw.shape[-1],
        w_a=w_a, w_u=w_u, w_q=w_q, w_kv=w_ukv[l].astype(BF),
        g_q=g_q[l][None], g_kv=g_kv[l][None],
        w_dw=w_dw[l], b_dw=b_dw[l][None], g_cn=g_cn[l][None], b_cn=b_cn[l][None],
        w_out=w_out[l].astype(BF), ln1_g=ln1_g[l][None], ln1_b=ln1_b[l][None],
        w_pq=w_pq[l].astype(BF), sub_keys=sub_keys[l].reshape(ph * two, nkeys, half).astype(BF),
        peer_u=peer_u[l].astype(BF), peer_v=peer_v[l].astype(BF),
        ln2_g=ln2_g[l][None], ln2_b=ln2_b[l][None],
    )


def kernel(x_prompt, x_sample, cache_ckv, cache_kpe, c, c_ctx, w_ada, b_ada, w_in, g_q, w_uq,
           g_kv, w_ukv, w_dw, b_dw, g_cn, b_cn, w_out, ln1_g, ln1_b, w_pq, sub_keys, peer_u,
           peer_v, ln2_g, ln2_b):
    depth = w_ada.shape[0]
    alpha = (2 * depth) ** 0.25
    B, T, D = x_prompt.shape
    Bd, Td, _ = x_sample.shape
    n_rows = -(-(1 + Bd) // SUBLANES) * SUBLANES
    cond = jnp.concatenate([c_ctx[None, :], c, jnp.zeros((n_rows - 1 - Bd, D), F32)], axis=0)

    n_tab = max(T, min(TOKEN_TILE, B * T))
    ones_tab = _lane_pad_table(jnp.ones((n_tab, ROPE_DIM), F32), jnp.zeros((n_tab, ROPE_DIM), F32))
    rope_tab = _rope_table(Td)

    xp, xs = x_prompt, x_sample
    ckv_layers, kpe_layers = [], []
    for l in range(depth):
        wts = _prep_weights(l, w_in, g_q, w_uq, g_kv, w_ukv, w_dw, b_dw, g_cn, b_cn, w_out,
                            ln1_g, ln1_b, w_pq, sub_keys, peer_u, peer_v, ln2_g, ln2_b)
        mod = _adaln(cond, w_ada[l], b_ada[l]).reshape(n_rows * 6, 1, D)
        xp, ckv, kpe = _path(xp, mod, lambda b: 0, True, ones_tab, wts, alpha)
        ckv_layers.append(ckv)
        kpe_layers.append(kpe)
        xs, _, _ = _path(xs, mod, lambda b: 1 + b, False, rope_tab, wts, alpha,
                         cache=(cache_ckv[:, l], cache_kpe[:, l]))
    return (xp, xs, jnp.stack(ckv_layers, axis=1), jnp.stack(kpe_layers, axis=1))
```

```python
import functools
import math

import jax
import jax.numpy as jnp
from jax import lax
from jax.experimental import pallas as pl
from jax.experimental.pallas import tpu as pltpu

F32 = jnp.float32
BF = jnp.bfloat16

NOPE_DIM = 128
ROPE_DIM = 64
V_DIM = 128
HEAD_Q = NOPE_DIM + 2 * ROPE_DIM
HEAD_KV = NOPE_DIM + V_DIM
GRID_W = 64
ROPE_BASE = 10000.0
PEER_TOPK = 16
EPS = 1e-6
LOG2_E = 1.4426950408889634

LANES = 128
SUBLANES = 8
VMEM_MIB = 1 << 20

TOKEN_TILE = 512
GLU_TOKEN_TILE = 1024
OUTPROJ_N_TILE = 512
PEER_TOKEN_TILE = 512
PEER_EXPERT_TILE = 1024
PEER_Q_GROUP = 4
ROUTE_HEAD_GROUP = 4
DOT_HEAD_GROUP = 4
ATTN_Q_TILE = 256
CONV_T_TILE = 256
CONV_T_CHUNK = 64
CONV_C_CHUNK = 128
CONV_HALO = 16


def _cp(sem, vmem_mib):
    return pltpu.CompilerParams(dimension_semantics=sem, vmem_limit_bytes=vmem_mib * VMEM_MIB)


def _rope_lanes(x, tab):
    q = ROPE_DIM // 4
    lane = lax.broadcasted_iota(jnp.int32, x.shape, 1)
    x = jnp.where(lane < ROPE_DIM, x, 0.0)
    nxt = pltpu.roll(x, LANES - q, 1)
    prv = pltpu.roll(x, q, 1)
    rot = jnp.where((lane // q) % 2 == 0, -nxt, prv)
    return x * tab[:, :LANES] + rot * tab[:, LANES:]


def _layer_norm_rows(z, g, b):
    mu = jnp.mean(z, -1, keepdims=True)
    zc = z - mu
    var = jnp.mean(zc * zc, -1, keepdims=True)
    return zc * lax.rsqrt(var + EPS) * g + b


def _rms_norm_rows(z, g):
    return z * lax.rsqrt(jnp.mean(z * z, -1, keepdims=True) + EPS) * g


def _adaln_kernel(c_ref, w_ref, b_ref, o_ref):
    c = c_ref[...]
    s = (c * jax.nn.sigmoid(c)).astype(BF)
    o_ref[...] = jnp.dot(s, w_ref[...].astype(BF), preferred_element_type=F32) + b_ref[...]


def _adaln(cond, w, b):
    R, D = cond.shape
    N = w.shape[1]
    tn = min(512, N)
    return pl.pallas_call(
        _adaln_kernel,
        grid=(N // tn,),
        in_specs=[pl.BlockSpec((R, D), lambda j: (0, 0)),
                  pl.BlockSpec((D, tn), lambda j: (0, j)),
                  pl.BlockSpec((1, tn), lambda j: (0, j))],
        out_specs=pl.BlockSpec((R, tn), lambda j: (0, j)),
        out_shape=jax.ShapeDtypeStruct((R, N), F32),
        compiler_params=_cp(("arbitrary",), 40),
        name="adaln",
    )(cond, w, b.reshape(1, N))


def _inproj_a_kernel(x_ref, sc_ref, sh_ref, w_ref, gq_ref, gkv_ref, rope_ref,
                     qcn_ref, ckv_ref, kpe_ref, kpad_ref, h_ref, *, qr, kvr):
    h_ref[...] = (x_ref[...] * (1.0 + sc_ref[0]) + sh_ref[0]).astype(BF)
    p = jnp.dot(h_ref[...], w_ref[...], preferred_element_type=F32)
    qcn_ref[...] = _rms_norm_rows(p[:, :qr], gq_ref[...]).astype(BF)
    ckv_ref[...] = _rms_norm_rows(p[:, qr:qr + kvr], gkv_ref[...])
    kslab = p[:, qr + kvr:qr + kvr + LANES]
    kpe_ref[...] = kslab[:, :ROPE_DIM]
    kpad_ref[...] = _rope_lanes(kslab, rope_ref[...]).astype(BF)


def _inproj_u_kernel(h_ref, wa_ref, wg_ref, y_ref):
    h = h_ref[...]
    a = jnp.dot(h, wa_ref[...], preferred_element_type=F32)
    g = jnp.dot(h, wg_ref[...], preferred_element_type=F32)
    y_ref[...] = a * jax.nn.sigmoid(g)


def _qproj_kernel(x_ref, w_ref, rope_ref, o_ref, *, heads):
    x = x_ref[...]
    rope = rope_ref[...]
    group = math.gcd(heads, DOT_HEAD_GROUP)
    for g in range(heads // group):
        qg = jnp.dot(x, w_ref[:, g * group * HEAD_Q:(g + 1) * group * HEAD_Q],
                     preferred_element_type=F32)
        for j in range(group):
            b = (g * group + j) * HEAD_Q
            q = qg[:, j * HEAD_Q:(j + 1) * HEAD_Q]
            o_ref[:, b:b + NOPE_DIM] = q[:, :NOPE_DIM].astype(BF)
            o_ref[:, b + NOPE_DIM:b + HEAD_Q] = _rope_lanes(q[:, NOPE_DIM:], rope).astype(BF)


def _kvproj_kernel(x_ref, w_ref, o_ref, *, heads):
    x = x_ref[...].astype(BF)
    width = math.gcd(heads, DOT_HEAD_GROUP) * HEAD_KV
    for g in range(heads * HEAD_KV // width):
        o_ref[:, g * width:(g + 1) * width] = jnp.dot(
            x, w_ref[:, g * width:(g + 1) * width], preferred_element_type=F32).astype(BF)


def _attn_kernel(q_ref, kv_ref, kpad_ref, o_ref, *, heads, scale):
    kpad = kpad_ref[0]
    for h in range(heads):
        qh = q_ref[0, :, h * HEAD_Q:(h + 1) * HEAD_Q]
        kh = jnp.concatenate([kv_ref[0, :, h * HEAD_KV:h * HEAD_KV + NOPE_DIM], kpad], axis=1)
        s = lax.dot_general(qh, kh, (((1,), (1,)), ((), ())), preferred_element_type=F32)
        m = jnp.max(s, -1, keepdims=True)
        p = jnp.exp2((s - m) * (scale * LOG2_E))
        l = jnp.sum(p, -1, keepdims=True)
        o = jnp.dot(p.astype(BF), kv_ref[0, :, h * HEAD_KV + NOPE_DIM:(h + 1) * HEAD_KV],
                    preferred_element_type=F32)
        o_ref[0, :, h * V_DIM:(h + 1) * V_DIM] = (o / l).astype(BF)


def _conv_kernel(y_ref, w_ref, bdw_ref, g_ref, b_ref, o_ref, pad_scr, z_scr, *, tt, taps):
    t = pl.program_id(1)
    nt = pl.num_programs(1)
    C = z_scr.shape[1]
    half = taps // 2
    pad_scr[CONV_HALO:CONV_HALO + tt, :] = y_ref[0, pl.ds(pl.multiple_of(t * tt, tt), tt), :]

    @pl.when(t == 0)
    def _():
        pad_scr[0:CONV_HALO, :] = jnp.zeros((CONV_HALO, C), F32)

    @pl.when(t > 0)
    def _():
        pad_scr[0:CONV_HALO, :] = y_ref[
            0, pl.ds(pl.multiple_of(t * tt - CONV_HALO, SUBLANES), CONV_HALO), :]

    @pl.when(t == nt - 1)
    def _():
        pad_scr[CONV_HALO + tt:, :] = jnp.zeros((CONV_HALO, C), F32)

    @pl.when(t < nt - 1)
    def _():
        pad_scr[CONV_HALO + tt:, :] = y_ref[
            0, pl.ds(pl.multiple_of((t + 1) * tt, tt), CONV_HALO), :]

    tc = CONV_T_CHUNK
    cc = min(CONV_C_CHUNK, C)
    base = CONV_HALO - half
    wlen = tc + 2 * CONV_HALO
    assert base >= 0 and base + taps - 1 + tc <= wlen

    def chan_body(ci, carry):
        c0 = pl.multiple_of(ci * cc, cc)
        for r in range(tt // tc):
            win = pad_scr[r * tc:r * tc + wlen, pl.ds(c0, cc)]
            acc = jnp.zeros((tc, cc), F32)
            for sh in range(SUBLANES):
                rolled = win if sh == 0 else pltpu.roll(win, wlen - sh, 0)
                for a in range(wlen // SUBLANES):
                    k = a * SUBLANES + sh - base
                    if 0 <= k < taps:
                        acc = acc + (rolled[a * SUBLANES:a * SUBLANES + tc, :]
                                     * w_ref[k:k + 1, pl.ds(c0, cc)])
            z_scr[r * tc:(r + 1) * tc, pl.ds(c0, cc)] = acc + bdw_ref[:, pl.ds(c0, cc)]
        return carry

    lax.fori_loop(0, C // cc, chan_body, 0)

    rows = 64
    for r in range(tt // rows):
        zn = _layer_norm_rows(z_scr[r * rows:(r + 1) * rows, :], g_ref[...], b_ref[...])
        o_ref[0, r * rows:(r + 1) * rows, :] = (zn * jax.nn.sigmoid(zn)).astype(BF)


def _ln_inplace(xo_ref, x_ref, gate_ref, lg_ref, lb_ref, ff_ref, alpha):
    rows = 64
    for r in range(xo_ref.shape[0] // rows):
        sl = slice(r * rows, (r + 1) * rows)
        z = alpha * x_ref[sl, :] + gate_ref[0] * ff_ref[sl, :]
        xo_ref[sl, :] = _layer_norm_rows(z, lg_ref[...], lb_ref[...])


def _outproj_kernel(a_ref, c_ref, wa_ref, wc_ref, x_ref, g1_ref, lg_ref, lb_ref, xo_ref,
                    *, tn, alpha):
    j = pl.program_id(1)
    col = pl.multiple_of(j * tn, tn)
    xo_ref[:, pl.ds(col, tn)] = (
        jnp.dot(a_ref[...], wa_ref[...], preferred_element_type=F32)
        + jnp.dot(c_ref[...], wc_ref[...], preferred_element_type=F32))

    @pl.when(j == pl.num_programs(1) - 1)
    def _():
        _ln_inplace(xo_ref, x_ref, g1_ref, lg_ref, lb_ref, xo_ref, alpha)


def _peer_q_kernel(x_ref, sc_ref, sh_ref, w_ref, k_ref, h_ref, s_ref):
    @pl.when(pl.program_id(1) == 0)
    def _():
        h_ref[...] = (x_ref[...] * (1.0 + sc_ref[0]) + sh_ref[0]).astype(BF)

    half = k_ref.shape[2]
    q = jnp.dot(h_ref[...], w_ref[...], preferred_element_type=F32).astype(BF)
    for p in range(k_ref.shape[0]):
        s_ref[p] = lax.dot_general(k_ref[p], q[:, p * half:(p + 1) * half],
                                   (((1,), (1,)), ((), ())), preferred_element_type=F32)


def _route_kernel(s_ref, st_ref, e2_ref, t2_scr, cand_scr):
    for hh in range(st_ref.shape[0]):
        _route_one_head(s_ref, st_ref, e2_ref, t2_scr, cand_scr, hh)


def _route_one_head(s_ref, st_ref, e2_ref, t2_scr, cand_scr, hh):
    n_top = PEER_TOPK + 1
    neg = -jnp.inf
    tl = s_ref.shape[2]

    def top(p):
        n_grp = s_ref.shape[1] // SUBLANES
        col = [s_ref[2 * hh + p, g * SUBLANES:(g + 1) * SUBLANES, :] for g in range(n_grp)]
        for i, j in _sorting_network(n_grp):
            col[i], col[j] = jnp.maximum(col[i], col[j]), jnp.minimum(col[i], col[j])
        col.append(jnp.full(col[0].shape, neg, F32))
        out = []
        for it in range(n_top):
            m = jnp.max(col[0], axis=0, keepdims=True)
            out.append(m)
            took = col[0] == m
            for g in range(min(n_grp, n_top - 1 - it)):
                col[g] = jnp.where(took, col[g + 1], col[g])
        return out

    t1 = top(0)
    t2 = top(1)
    for k in range(n_top):
        t2_scr[k:k + 1, :] = t2[k]
    cand_scr[...] = jnp.full(cand_scr.shape, neg, F32)
    off = 0
    for a in range(n_top):
        nb = n_top // (a + 1)
        cand_scr[off:off + nb, :] = t2_scr[0:nb, :] + t1[a]
        off += nb
    cand = cand_scr[...]
    c = cand
    kth = None
    nxt = None
    for it in range(n_top):
        m = jnp.max(c, axis=0, keepdims=True)
        if it == PEER_TOPK - 1:
            kth = m
        if it == PEER_TOPK:
            nxt = m
        c = jnp.where(c == m, neg, c)
    tau = 0.5 * (kth + nxt)
    top_sum = t1[0] + t2[0]
    z = jnp.sum(jnp.where(cand > tau, jnp.exp(cand - top_sum), 0.0), axis=0, keepdims=True)
    st_ref[hh, 0:1, :] = tau
    st_ref[hh, 1:2, :] = t1[0]
    log_norm = t2[0] + jnp.log(z) + math.log(2.0)
    st_ref[hh, 2:3, :] = log_norm
    st_ref[hh, 3:, :] = jnp.zeros((SUBLANES - 3, tl), F32)
    e2_ref[hh] = jnp.exp(s_ref[2 * hh + 1] - log_norm)


def _sorting_network(n):
    assert n & (n - 1) == 0

    def merge(lo, hi, r):
        step = r * 2
        if step < hi - lo:
            yield from merge(lo, hi, step)
            yield from merge(lo + r, hi, step)
            yield from ((i, i + r) for i in range(lo + r, hi - r, step))
        else:
            yield (lo, lo + r)

    def sort(lo, hi):
        if hi - lo >= 1:
            mid = lo + (hi - lo) // 2
            yield from sort(lo, mid)
            yield from sort(mid + 1, hi)
            yield from merge(lo, hi, 1)

    return list(sort(0, n - 1))


def _n_cand(n_top):
    return sum(n_top // (a + 1) for a in range(n_top))


def _twice_gelu(x):
    return x * (1.0 + lax.erf(x * (1.0 / math.sqrt(2.0))))


def _peer_kernel(h_ref, s1_ref, e2_ref, st_ref, u_ref, v_ref, o_ref, a_scr, w_scr, *, heads, keys):
    nt = h_ref.shape[0]
    et = u_ref.shape[0]
    n_halves = 2 if nt >= 2 * LANES else 1
    half = nt // n_halves

    @pl.when(pl.program_id(1) == 0)
    def _():
        o_ref[...] = jnp.zeros(o_ref.shape, F32)

    def weight_block(ii, c):
        cols = slice(c * LANES, (c + 1) * LANES)
        w = jnp.zeros((keys, LANES), F32)
        for h in range(heads):
            s1 = s1_ref[h, ii:ii + 1, cols]
            e2_min = jnp.exp(st_ref[h, 0:1, cols] - s1 - st_ref[h, 2:3, cols])
            e1 = jnp.exp(s1 - st_ref[h, 1:2, cols])
            e2 = e2_ref[h, :, cols]
            w = w + jnp.where(e2 >= e2_min, e2, 0.0) * e1
        ars = slice(ii * keys, (ii + 1) * keys)
        w_scr[cols, ars] = (w * _twice_gelu(a_scr[ars, cols])).T.astype(BF)

    def up_rows(k):
        rs = slice(k * half, (k + 1) * half)
        w_rows = w_scr[rs, :]
        nc = min(512, o_ref.shape[1])
        for n in range(o_ref.shape[1] // nc):
            cs = slice(n * nc, (n + 1) * nc)
            o_ref[rs, cs] += jnp.dot(w_rows, v_ref[:, cs], preferred_element_type=F32)

    def down_cols(k):
        cols = slice(k * half, (k + 1) * half)
        a_scr[:, cols] = lax.dot_general(u_ref[...], h_ref[cols, :], (((1,), (1,)), ((), ())),
                                         preferred_element_type=F32)

    for k in range(n_halves):
        down_cols(k)
    for k in range(n_halves):
        for ii in range(et // keys):
            for c in range(k * half // LANES, (k + 1) * half // LANES):
                weight_block(ii, c)
        up_rows(k)


def _ln2_kernel(x_ref, ff_ref, g2_ref, lg_ref, lb_ref, o_ref, *, alpha):
    _ln_inplace(o_ref, x_ref, g2_ref, lg_ref, lb_ref, ff_ref, alpha)


def _mod_spec(D, which, row_fn):
    return pl.BlockSpec((1, 1, D), lambda i, *_: (row_fn(i) * 6 + which, 0, 0))


def _path(x, mod, row_of_seq, shared_mod, rope_tab, wts, alpha, cache=None):
    B, T, D = x.shape
    M = B * T
    x2 = x.reshape(M, D)
    heads = wts["heads"]
    qr, kvr = wts["qr"], wts["kvr"]
    CW = wts["cw"]
    MW = heads * V_DIM
    tm = min(TOKEN_TILE, M) if shared_mod else min(TOKEN_TILE, T)
    assert M % tm == 0 and (T % tm == 0 or tm % T == 0)
    assert rope_tab.shape[0] == max(T, tm)

    def row_fn(i):
        return row_of_seq((i * tm) // T)

    def rope_idx(i, *_):
        return (i % max(T // tm, 1), 0)

    mspec = functools.partial(_mod_spec, D, row_fn=row_fn)
    row_spec = pl.BlockSpec((tm, D), lambda i, *_: (i, 0))

    wa = wts["w_a"]
    NA = wa.shape[1]
    qcn, ckv, kpe, kpad, h1 = pl.pallas_call(
        functools.partial(_inproj_a_kernel, qr=qr, kvr=kvr),
        grid=(M // tm,),
        in_specs=[row_spec, mspec(1), mspec(0),
                  pl.BlockSpec((D, NA), lambda i: (0, 0), pipeline_mode=pl.Buffered(1)),
                  pl.BlockSpec((1, qr), lambda i: (0, 0)),
                  pl.BlockSpec((1, kvr), lambda i: (0, 0)),
                  pl.BlockSpec((tm, 2 * LANES), rope_idx)],
        out_specs=[pl.BlockSpec((tm, qr), lambda i: (i, 0)),
                   pl.BlockSpec((tm, kvr), lambda i: (i, 0)),
                   pl.BlockSpec((tm, ROPE_DIM), lambda i: (i, 0)),
                   pl.BlockSpec((tm, LANES), lambda i: (i, 0)),
                   row_spec],
        out_shape=[jax.ShapeDtypeStruct((M, qr), BF),
                   jax.ShapeDtypeStruct((M, kvr), F32),
                   jax.ShapeDtypeStruct((M, ROPE_DIM), F32),
                   jax.ShapeDtypeStruct((M, LANES), BF),
                   jax.ShapeDtypeStruct((M, D), BF)],
        compiler_params=_cp(("arbitrary",), 56),
        name="inproj_a",
    )(x2, mod, mod, wa, wts["g_q"], wts["g_kv"], rope_tab)

    wu = wts["w_u"]
    tn = min(512, CW)
    tmu = math.gcd(M, GLU_TOKEN_TILE)
    y = pl.pallas_call(
        _inproj_u_kernel,
        grid=(M // tmu, CW // tn),
        in_specs=[pl.BlockSpec((tmu, D), lambda i, j: (i, 0)),
                  pl.BlockSpec((D, tn), lambda i, j: (0, j)),
                  pl.BlockSpec((D, tn), lambda i, j: (0, j + CW // tn))],
        out_specs=pl.BlockSpec((tmu, tn), lambda i, j: (i, j)),
        out_shape=jax.ShapeDtypeStruct((M, CW), F32),
        compiler_params=_cp(("arbitrary", "arbitrary"), 56),
        name="inproj_u",
    )(h1, wu, wu)

    wq = wts["w_q"]
    q = pl.pallas_call(
        functools.partial(_qproj_kernel, heads=heads),
        grid=(M // tm,),
        in_specs=[pl.BlockSpec((tm, qr), lambda i: (i, 0)),
                  pl.BlockSpec(wq.shape, lambda i: (0, 0)),
                  pl.BlockSpec((tm, 2 * LANES), rope_idx)],
        out_specs=pl.BlockSpec((tm, heads * HEAD_Q), lambda i: (i, 0)),
        out_shape=jax.ShapeDtypeStruct((M, heads * HEAD_Q), BF),
        compiler_params=_cp(("arbitrary",), 48),
        name="qproj",
    )(qcn, wq, rope_tab)

    if cache is not None:
        cache_ckv, cache_kpe = cache
        P = cache_ckv.shape[1]
        ckv_all = jnp.concatenate([cache_ckv, ckv.reshape(B, T, kvr)], axis=1)
        kpad_all = jnp.concatenate(
            [jnp.pad(cache_kpe, ((0, 0), (0, 0), (0, LANES - ROPE_DIM))).astype(BF),
             kpad.reshape(B, T, LANES)], axis=1)
    else:
        P = 0
        ckv_all = ckv.reshape(B, T, kvr)
        kpad_all = kpad.reshape(B, T, LANES)
    S = P + T
    MS = B * S
    tk = math.gcd(MS, TOKEN_TILE)
    wkv = wts["w_kv"]
    kv = pl.pallas_call(
        functools.partial(_kvproj_kernel, heads=heads),
        grid=(MS // tk,),
        in_specs=[pl.BlockSpec((tk, kvr), lambda i: (i, 0)),
                  pl.BlockSpec(wkv.shape, lambda i: (0, 0))],
        out_specs=pl.BlockSpec((tk, heads * HEAD_KV), lambda i: (i, 0)),
        out_shape=jax.ShapeDtypeStruct((MS, heads * HEAD_KV), BF),
        compiler_params=_cp(("arbitrary",), 40),
        name="kvproj",
    )(ckv_all.reshape(MS, kvr), wkv)

    tq = min(ATTN_Q_TILE, T)
    attn = pl.pallas_call(
        functools.partial(_attn_kernel, heads=heads, scale=(NOPE_DIM + ROPE_DIM) ** -0.5),
        grid=(B, T // tq),
        in_specs=[pl.BlockSpec((1, tq, heads * HEAD_Q), lambda b, t: (b, t, 0)),
                  pl.BlockSpec((1, S, heads * HEAD_KV), lambda b, t: (b, 0, 0)),
                  pl.BlockSpec((1, S, LANES), lambda b, t: (b, 0, 0))],
        out_specs=pl.BlockSpec((1, tq, MW), lambda b, t: (b, t, 0)),
        out_shape=jax.ShapeDtypeStruct((B, T, MW), BF),
        compiler_params=_cp(("arbitrary", "arbitrary"), 48),
        name="attn",
    )(q.reshape(B, T, heads * HEAD_Q), kv.reshape(B, S, heads * HEAD_KV), kpad_all)

    tt = min(CONV_T_TILE, T)
    taps = wts["w_dw"].shape[0]
    conv = pl.pallas_call(
        functools.partial(_conv_kernel, tt=tt, taps=taps),
        grid=(B, T // tt),
        in_specs=[pl.BlockSpec((1, T, CW), lambda b, t: (b, 0, 0)),
                  pl.BlockSpec((taps, CW), lambda b, t: (0, 0)),
                  pl.BlockSpec((1, CW), lambda b, t: (0, 0)),
                  pl.BlockSpec((1, CW), lambda b, t: (0, 0)),
                  pl.BlockSpec((1, CW), lambda b, t: (0, 0))],
        out_specs=pl.BlockSpec((1, tt, CW), lambda b, t: (b, t, 0)),
        out_shape=jax.ShapeDtypeStruct((B, T, CW), BF),
        scratch_shapes=[pltpu.VMEM((tt + 2 * CONV_HALO, CW), F32), pltpu.VMEM((tt, CW), F32)],
        compiler_params=_cp(("arbitrary", "arbitrary"), 40),
        name="conv",
    )(y.reshape(B, T, CW), wts["w_dw"], wts["b_dw"], wts["g_cn"], wts["b_cn"])

    wo = wts["w_out"]
    tno = min(OUTPROJ_N_TILE, D)
    assert MW % CW == 0
    vec_spec = pl.BlockSpec((1, D), lambda i, *_: (0, 0))
    x1 = pl.pallas_call(
        functools.partial(_outproj_kernel, tn=tno, alpha=alpha),
        grid=(M // tm, D // tno),
        in_specs=[pl.BlockSpec((tm, MW), lambda i, j: (i, 0)),
                  pl.BlockSpec((tm, CW), lambda i, j: (i, 0)),
                  pl.BlockSpec((MW, tno), lambda i, j: (0, j)),
                  pl.BlockSpec((CW, tno), lambda i, j: (MW // CW, j)),
                  row_spec, mspec(2), vec_spec, vec_spec],
        out_specs=row_spec,
        out_shape=jax.ShapeDtypeStruct((M, D), F32),
        compiler_params=_cp(("arbitrary", "arbitrary"), 60),
        name="outproj",
    )(attn.reshape(M, MW), conv.reshape(M, CW), wo, wo, x2, mod, wts["ln1_g"], wts["ln1_b"])

    wpq = wts["w_pq"]
    keys_bf = wts["sub_keys"]
    nhp, nkeys, half = keys_bf.shape
    ph = nhp // 2
    gq = math.gcd(nhp, PEER_Q_GROUP)
    h2, sT = pl.pallas_call(
        _peer_q_kernel,
        grid=(M // tm, nhp // gq),
        in_specs=[row_spec, mspec(4), mspec(3),
                  pl.BlockSpec((D, gq * half), lambda i, j: (0, j)),
                  pl.BlockSpec((gq, nkeys, half), lambda i, j: (j, 0, 0))],
        out_specs=[pl.BlockSpec((tm, D), lambda i, j: (i, 0)),
                   pl.BlockSpec((gq, nkeys, tm), lambda i, j: (j, 0, i))],
        out_shape=[jax.ShapeDtypeStruct((M, D), BF),
                   jax.ShapeDtypeStruct((nhp, nkeys, M), F32)],
        compiler_params=_cp(("arbitrary", "arbitrary"), 48),
        name="peer_q",
    )(x1, mod, mod, wpq, keys_bf)

    tl = min(256, M)
    gh = math.gcd(ph, ROUTE_HEAD_GROUP)
    stats, e2 = pl.pallas_call(
        _route_kernel,
        grid=(M // tl, ph // gh),
        in_specs=[pl.BlockSpec((2 * gh, nkeys, tl), lambda i, h: (h, 0, i))],
        out_specs=[pl.BlockSpec((gh, SUBLANES, tl), lambda i, h: (h, 0, i)),
                   pl.BlockSpec((gh, nkeys, tl), lambda i, h: (h, 0, i))],
        out_shape=[jax.ShapeDtypeStruct((ph, SUBLANES, M), F32),
                   jax.ShapeDtypeStruct((ph, nkeys, M), F32)],
        scratch_shapes=[pltpu.VMEM((24, tl), F32),
                        pltpu.VMEM((-(-_n_cand(PEER_TOPK + 1) // SUBLANES) * SUBLANES, tl), F32)],
        compiler_params=_cp(("arbitrary", "arbitrary"), 32),
        name="peer_route",
    )(sT)

    pu, pv = wts["peer_u"], wts["peer_v"]
    NE = pu.shape[0]
    nt = min(PEER_TOKEN_TILE, M)
    et = min(PEER_EXPERT_TILE, NE)
    rows = et // nkeys
    assert et % nkeys == 0 and NE == nkeys * nkeys and rows % SUBLANES == 0
    ff = pl.pallas_call(
        functools.partial(_peer_kernel, heads=ph, keys=nkeys),
        grid=(M // nt, NE // et),
        in_specs=[pl.BlockSpec((nt, D), lambda i, e: (i, 0)),
                  pl.BlockSpec((ph, None, rows, nt), lambda i, e: (0, 0, e, i)),
                  pl.BlockSpec((ph, nkeys, nt), lambda i, e: (0, 0, i)),
                  pl.BlockSpec((ph, SUBLANES, nt), lambda i, e: (0, 0, i)),
                  pl.BlockSpec((et, D), lambda i, e: (e, 0)),
                  pl.BlockSpec((et, D), lambda i, e: (e, 0))],
        out_specs=pl.BlockSpec((nt, D), lambda i, e: (i, 0), pipeline_mode=pl.Buffered(1)),
        out_shape=jax.ShapeDtypeStruct((M, D), F32),
        scratch_shapes=[pltpu.VMEM((et, nt), F32),
                        pltpu.VMEM((nt, et), BF)],
        compiler_params=_cp(("arbitrary", "arbitrary"), 60),
        name="peer_main",
    )(h2, sT.reshape(ph, 2, nkeys, M), e2, stats, pu, pv)

    t2 = min(256, T)

    def row_fn2(i):
        return row_of_seq((i * t2) // T)

    row2 = pl.BlockSpec((t2, D), lambda i: (i, 0))
    y_out = pl.pallas_call(
        functools.partial(_ln2_kernel, alpha=alpha),
        grid=(M // t2,),
        in_specs=[row2, row2, _mod_spec(D, 5, row_fn2),
                  pl.BlockSpec((1, D), lambda i: (0, 0)), pl.BlockSpec((1, D), lambda i: (0, 0))],
        out_specs=row2,
        out_shape=jax.ShapeDtypeStruct((M, D), F32),
        compiler_params=_cp(("arbitrary",), 40),
        name="ln2",
    )(x1, ff, mod, wts["ln2_g"], wts["ln2_b"])

    return y_out.reshape(B, T, D), ckv.reshape(B, T, kvr), kpe.reshape(B, T, ROPE_DIM)


def _rope_table(n_tokens):
    rows = n_tokens // GRID_W
    row = jnp.repeat(jnp.arange(rows, dtype=F32), GRID_W)
    col = jnp.tile(jnp.arange(GRID_W, dtype=F32), rows)
    n_freq = ROPE_DIM // 4
    inv = ROPE_BASE ** (-jnp.arange(n_freq, dtype=F32) / n_freq)
    ang_r = row[:, None] * inv
    ang_c = col[:, None] * inv
    ang = jnp.concatenate([ang_r, ang_r, ang_c, ang_c], -1)
    return _lane_pad_table(jnp.cos(ang), jnp.sin(ang))


def _lane_pad_table(cos, sin):
    zero = jnp.zeros((cos.shape[0], LANES - ROPE_DIM), F32)
    return jnp.concatenate([cos, zero, sin, zero], -1)


def _prep_weights(l, w_in, g_q, w_uq, g_kv, w_ukv, w_dw, b_dw, g_cn, b_cn, w_out,
                  ln1_g, ln1_b, w_pq, sub_keys, peer_u, peer_v, ln2_g, ln2_b):
    qr = g_q.shape[-1]
    kvr = g_kv.shape[-1]
    o3 = qr + kvr + ROPE_DIM
    heads = w_uq.shape[-1] // (NOPE_DIM + ROPE_DIM)
    wi = w_in[l]
    n_a = qr + kvr + LANES
    assert (qr + kvr) % LANES == 0 and n_a <= wi.shape[1]
    wi_bf = wi.astype(BF)
    w_a = wi_bf[:, :n_a]
    w_u = wi_bf[:, o3:]
    wq3 = w_uq[l].reshape(qr, heads, NOPE_DIM + ROPE_DIM).astype(BF)
    w_q = jnp.pad(wq3, ((0, 0), (0, 0), (0, HEAD_Q - NOPE_DIM - ROPE_DIM)))
    w_q = w_q.reshape(qr, heads * HEAD_Q)
    ph, two, nkeys, half = sub_keys.shape[1:]
    return dict(
        heads=heads, qr=qr, kvr=kvr, cw=w_dw.shape[-1],
        w_a=w_a, w_u=w_u, w_q=w_q, w_kv=w_ukv[l].astype(BF),
        g_q=g_q[l][None], g_kv=g_kv[l][None],
        w_dw=w_dw[l], b_dw=b_dw[l][None], g_cn=g_cn[l][None], b_cn=b_cn[l][None],
        w_out=w_out[l].astype(BF), ln1_g=ln1_g[l][None], ln1_b=ln1_b[l][None],
        w_pq=w_pq[l].astype(BF), sub_keys=sub_keys[l].reshape(ph * two, nkeys, half).astype(BF),
        peer_u=peer_u[l].astype(BF), peer_v=peer_v[l].astype(BF),
        ln2_g=ln2_g[l][None], ln2_b=ln2_b[l][None],
    )


def kernel(x_prompt, x_sample, cache_ckv, cache_kpe, c, c_ctx, w_ada, b_ada, w_in, g_q, w_uq,
           g_kv, w_ukv, w_dw, b_dw, g_cn, b_cn, w_out, ln1_g, ln1_b, w_pq, sub_keys, peer_u,
           peer_v, ln2_g, ln2_b):
    depth = w_ada.shape[0]
    alpha = (2 * depth) ** 0.25
    B, T, D = x_prompt.shape
    Bd, Td, _ = x_sample.shape
    n_rows = -(-(1 + Bd) // SUBLANES) * SUBLANES
    cond = jnp.concatenate([c_ctx[None, :], c, jnp.zeros((n_rows - 1 - Bd, D), F32)], axis=0)

    n_tab = max(T, min(TOKEN_TILE, B * T))
    ones_tab = _lane_pad_table(jnp.ones((n_tab, ROPE_DIM), F32), jnp.zeros((n_tab, ROPE_DIM), F32))
    rope_tab = _rope_table(Td)

    xp, xs = x_prompt, x_sample
    ckv_layers, kpe_layers = [], []
    for l in range(depth):
        wts = _prep_weights(l, w_in, g_q, w_uq, g_kv, w_ukv, w_dw, b_dw, g_cn, b_cn, w_out,
                            ln1_g, ln1_b, w_pq, sub_keys, peer_u, peer_v, ln2_g, ln2_b)
        mod = _adaln(cond, w_ada[l], b_ada[l]).reshape(n_rows * 6, 1, D)
        xp, ckv, kpe = _path(xp, mod, lambda b: 0, True, ones_tab, wts, alpha)
        ckv_layers.append(ckv)
        kpe_layers.append(kpe)
        xs, _, _ = _path(xs, mod, lambda b: 1 + b, False, rope_tab, wts, alpha,
                         cache=(cache_ckv[:, l], cache_kpe[:, l]))
    return (xp, xs, jnp.stack(ckv_layers, axis=1), jnp.stack(kpe_layers, axis=1))
```
